```python
import jax, jax.numpy as jnp
from jax import lax
import numpy as np

D_MODEL = 1024
BATCH = 4
SEQ = 4096
DEPTH = 1
DEC_BATCH = 4
DEC_SEQ = 8192
PAST_LEN = 128

HEAD_DIM = 128
N_HEADS_A = 8
N_HEADS_F = 8
D_A = N_HEADS_A * HEAD_DIM
D_F = N_HEADS_F * HEAD_DIM
D_MIX = D_A + D_F
D_IN = 4 * D_A + 2 * D_F
CONV_W = 3
ALPHA = float((2 * DEPTH) ** 0.25)
BETA = float((8 * DEPTH) ** -0.25)
LN_EPS = 1e-5

kernel_name = "hybrid_shortconv_fnet_encoder"


def _layernorm(r, g, b):
    rf = r.astype(jnp.float32)
    mu = jnp.mean(rf, axis=-1, keepdims=True)
    var = jnp.mean(jnp.square(rf - mu), axis=-1, keepdims=True)
    y = ((rf - mu) * lax.rsqrt(var + LN_EPS)).astype(r.dtype)
    return y * g + b


def _layer(x, c, w_ada, b_ada, w_in, conv_w, conv_b, w_fmix, w_out, ln_g, ln_b):
    bsz, seq, _ = x.shape
    mod = jax.nn.silu(c) @ w_ada + b_ada
    shift, scale, gate = jnp.split(mod, 3, axis=-1)
    h = x * (1.0 + scale[:, None, :]) + shift[:, None, :]

    z = h @ w_in
    bg, cg, va, ga, vf, gf = jnp.split(
        z, [D_A, 2 * D_A, 3 * D_A, 4 * D_A, 4 * D_A + D_F], axis=-1)

    u = cg * va
    up = jnp.pad(u, ((0, 0), (1, 1), (0, 0)))
    conv = (up[:, :-2] * conv_w[0] + up[:, 1:-1] * conv_w[1]
            + up[:, 2:] * conv_w[2] + conv_b)
    y_a = bg * conv * jax.nn.silu(ga)

    vg = vf.reshape(bsz, seq, N_HEADS_F, HEAD_DIM).astype(jnp.float32)
    fr = jnp.real(jnp.fft.fftn(vg, axes=(1, 3), norm="ortho")).astype(x.dtype)
    fr = jnp.einsum('bshd,hde->bshe', fr, w_fmix).reshape(bsz, seq, D_F)
    y_f = fr * jax.nn.silu(gf)

    o = jnp.concatenate([y_a, y_f], axis=-1) @ w_out
    return _layernorm(ALPHA * x + gate[:, None, :] * o, ln_g, ln_b)


def _trunk(x, c, w_ada, b_ada, w_in, conv_w, conv_b, w_fmix, w_out, ln_g, ln_b):
    for l in range(DEPTH):
        x = _layer(x, c, w_ada[l], b_ada[l], w_in[l], conv_w[l], conv_b[l],
                   w_fmix[l], w_out[l], ln_g[l], ln_b[l])
    return x


def setup_inputs(seed: int = 0) -> dict:
    key = jax.random.key(seed)
    ks = jax.random.split(key, 14)
    f32 = jnp.float32
    x_prompt = jax.random.normal(ks[0], (BATCH, SEQ, D_MODEL), f32)
    x_sample = jax.random.normal(ks[1], (DEC_BATCH, DEC_SEQ, D_MODEL), f32)
    c_prompt = jax.random.normal(ks[2], (BATCH, D_MODEL), f32)
    c_sample = jax.random.normal(ks[3], (DEC_BATCH, D_MODEL), f32)
    w_ada = jax.random.normal(ks[4], (DEPTH, D_MODEL, 3 * D_MODEL), f32) * (0.1 * D_MODEL ** -0.5)
    b_ada = jax.random.normal(ks[5], (DEPTH, 3 * D_MODEL), f32) * 0.02
    w_in = jax.random.normal(ks[6], (DEPTH, D_MODEL, D_IN), f32) * D_MODEL ** -0.5
    conv_w = jax.random.normal(ks[7], (DEPTH, CONV_W, D_A), f32) * CONV_W ** -0.5
    conv_b = jax.random.normal(ks[8], (DEPTH, D_A), f32) * 0.02
    w_fmix = jax.random.normal(ks[9], (DEPTH, N_HEADS_F, HEAD_DIM, HEAD_DIM), f32) * HEAD_DIM ** -0.5
    w_out = jax.random.normal(ks[10], (DEPTH, D_MIX, D_MODEL), f32) * (D_MIX ** -0.5 * BETA)
    ln_g = 1.0 + 0.02 * jax.random.normal(ks[11], (DEPTH, D_MODEL), f32)
    ln_b = 0.02 * jax.random.normal(ks[12], (DEPTH, D_MODEL), f32)
    return {"x_prompt": x_prompt, "x_sample": x_sample, "c_prompt": c_prompt, "c_sample": c_sample,
            "w_ada": w_ada, "b_ada": b_ada, "w_in": w_in, "conv_w": conv_w, "conv_b": conv_b,
            "w_fmix": w_fmix, "w_out": w_out, "ln_g": ln_g, "ln_b": ln_b}


def reference(x_prompt, x_sample, c_prompt, c_sample, w_ada, b_ada, w_in, conv_w, conv_b,
              w_fmix, w_out, ln_g, ln_b):
    y_prompt = _trunk(x_prompt, c_prompt, w_ada, b_ada, w_in, conv_w, conv_b, w_fmix, w_out, ln_g, ln_b)
    y_sample = _trunk(x_sample, c_sample, w_ada, b_ada, w_in, conv_w, conv_b, w_fmix, w_out, ln_g, ln_b)
    return (y_prompt, y_sample)
```

```python
import functools

import numpy as np
import jax
import jax.numpy as jnp
from jax import lax
from jax.experimental import pallas as pl
from jax.experimental.pallas import tpu as pltpu

D_MODEL = 1024
HEAD_DIM = 128
N_GROUPS = 8
D_A = 1024
D_F = 1024
ALPHA = float(2.0 ** 0.25)
LN_EPS = 1e-5

CHUNK = 16
FFT_LANES = 256
VMEM_LIMIT = 56 * 1024 * 1024


def _fft_constants(seq):
    r2 = seq // 256
    h = r2 // 16
    eye = np.eye(16)
    d = np.arange(16)
    e0 = np.arange(r2)
    c1 = np.exp(-2j * np.pi * np.outer(e0, np.arange(r2)) / r2)
    m1c = np.einsum('ab,ed->aedb', eye, c1).reshape(16 * r2, 16 * r2)
    m1 = np.concatenate([m1c.real, m1c.imag], axis=0)

    m2 = []
    for e0h in range(h):
        ph = (d[None, None, :] * d[:, None, None]) / 16.0 \
            + d[None, None, :] * (d[None, :, None] + 16 * e0h) / (16.0 * r2)
        c = np.exp(-2j * np.pi * ph)
        cf = np.einsum('ald,lm->aldm', c, eye).reshape(256, 256)
        m2.append(np.block([[cf.real, -cf.imag], [cf.imag, cf.real]]))
    m2 = np.stack(m2)

    ph = (d[None, None, :] * d[:, None, None]) / 16.0 + d[None, None, :] * d[None, :, None] / float(seq)
    c = np.exp(-2j * np.pi * ph) / np.sqrt(seq)
    cf = np.einsum('ald,lm->aldm', c, eye).reshape(256, 256)
    m3 = np.block([[cf.real, -cf.imag], [cf.imag, cf.real]])

    j = np.arange(r2)
    tw = np.exp(-2j * np.pi * (j // h)[:, None] * (16 * (j % h)[:, None] + r2 * d[None, :]) / seq)
    f32 = np.float32
    return (m1.astype(f32), m2.astype(f32), m3.astype(f32),
            tw.real.reshape(-1).astype(f32), tw.imag.reshape(-1).astype(f32))


def _channel_dft():
    n = np.arange(HEAD_DIM)
    ang = 2 * np.pi * np.outer(n, n) / HEAD_DIM
    cs = np.concatenate([np.cos(ang), np.sin(ang)], axis=0) / np.sqrt(HEAD_DIM)
    return cs.astype(np.float32)


def _mod_kernel(c_ref, w_ref, b_ref, o_ref):
    c = c_ref[...]
    o_ref[...] = jnp.dot(jax.nn.silu(c), w_ref[...], precision=lax.Precision.HIGHEST,
                         preferred_element_type=jnp.float32) + b_ref[...]


def _modulation(c_all, w_ada, b_ada):
    n = c_all.shape[0]
    return pl.pallas_call(
        _mod_kernel,
        out_shape=jax.ShapeDtypeStruct((n, 3 * D_MODEL), jnp.float32),
        compiler_params=pltpu.CompilerParams(vmem_limit_bytes=VMEM_LIMIT),
        name="adaln_mod",
    )(c_all, w_ada, b_ada.reshape(1, -1))


def _fold_kernel(cs_ref, w_ref, o_ref):
    for g in range(N_GROUPS):
        o_ref[g] = jnp.dot(cs_ref[...], w_ref[g], precision=lax.Precision.HIGHEST,
                           preferred_element_type=jnp.float32).astype(o_ref.dtype)


def _fold_fmix(w_fmix):
    cs = jnp.asarray(_channel_dft())
    return pl.pallas_call(
        _fold_kernel,
        out_shape=jax.ShapeDtypeStruct((N_GROUPS, 2 * HEAD_DIM, HEAD_DIM), jnp.bfloat16),
        name="fold_fmix",
    )(cs, w_fmix)


def _vf_kernel(x_ref, scale_ref, shift_ref, w_ref, o_ref):
    h = x_ref[0] * (1.0 + scale_ref[0]) + shift_ref[0]
    o_ref[0] = jnp.dot(h.astype(jnp.bfloat16), w_ref[...],
                       preferred_element_type=jnp.float32).astype(o_ref.dtype)


def _vf_proj(x, scale, shift, w_vf, tm):
    b, s, d = x.shape
    return pl.pallas_call(
        _vf_kernel,
        grid=(b, s // tm),
        in_specs=[
            pl.BlockSpec((1, tm, d), lambda i, j: (i, j, 0)),
            pl.BlockSpec((1, 1, d), lambda i, j: (i, 0, 0)),
            pl.BlockSpec((1, 1, d), lambda i, j: (i, 0, 0)),
            pl.BlockSpec((d, D_F), lambda i, j: (0, 0)),
        ],
        out_specs=pl.BlockSpec((1, tm, D_F), lambda i, j: (i, j, 0)),
        out_shape=jax.ShapeDtypeStruct((b, s, D_F), jnp.bfloat16),
        compiler_params=pltpu.CompilerParams(
            dimension_semantics=("arbitrary", "arbitrary"), vmem_limit_bytes=VMEM_LIMIT),
        name="vf_proj",
    )(x, scale, shift, w_vf)


def _fft_kernel(v_ref, m1_ref, m2_ref, m3_ref, twr_ref, twi_ref, ur_ref, ui_ref, y2_ref, *, r2):
    h = r2 // 16
    k1 = 16 * r2
    bf16 = jnp.bfloat16

    def stage1(d1, carry):
        rhs = jnp.concatenate(
            [v_ref[0, pl.ds(pl.multiple_of(256 * d2 + CHUNK * d1, CHUNK), CHUNK), :] for d2 in range(r2)], axis=0)
        r = jnp.dot(m1_ref[...], rhs, preferred_element_type=jnp.float32)
        row = pl.multiple_of(d1 * k1, CHUNK)
        ur_ref[0, pl.ds(row, k1), :] = r[:k1].astype(bf16)
        ui_ref[0, pl.ds(row, k1), :] = r[k1:].astype(bf16)
        return carry

    lax.fori_loop(0, 16, stage1, 0)

    def stage2(j, carry):
        off = pl.multiple_of(CHUNK * j, CHUNK)
        parts = []
        for ref in (ur_ref, ui_ref):
            for d1 in range(16):
                parts.append(ref[0, pl.ds(d1 * k1 + off, CHUNK), :])
        rhs = jnp.concatenate(parts, axis=0)
        m2 = m2_ref[0] if h == 1 else m2_ref[lax.rem(j, h)]
        r = jnp.dot(m2, rhs, preferred_element_type=jnp.float32)
        for e1 in range(16):
            wr = twr_ref[j * 16 + e1]
            wi = twi_ref[j * 16 + e1]
            a = r[CHUNK * e1:CHUNK * (e1 + 1)]
            b = r[256 + CHUNK * e1:256 + CHUNK * (e1 + 1)]
            y2_ref[j, 0, CHUNK * e1:CHUNK * (e1 + 1), :] = (a * wr - b * wi).astype(bf16)
            y2_ref[j, 1, CHUNK * e1:CHUNK * (e1 + 1), :] = (a * wi + b * wr).astype(bf16)
        return carry

    lax.fori_loop(0, r2, stage2, 0)

    def stage3(q, carry):
        if h == 1:
            e0h, e1 = 0, q
        else:
            e0h, e1 = lax.rem(q, h), lax.div(q, h)
        off = pl.multiple_of(CHUNK * e1, CHUNK)
        parts = []
        for p in range(2):
            for d0 in range(16):
                parts.append(y2_ref[d0 * h + e0h, p, pl.ds(off, CHUNK), :])
        rhs = jnp.concatenate(parts, axis=0)
        r = jnp.dot(m3_ref[...], rhs, preferred_element_type=jnp.float32).astype(bf16)
        for e2 in range(16):
            row = pl.multiple_of(CHUNK * r2 * e2 + CHUNK * q, CHUNK)
            ur_ref[0, pl.ds(row, CHUNK), :] = r[CHUNK * e2:CHUNK * (e2 + 1)]
            ui_ref[0, pl.ds(row, CHUNK), :] = r[256 + CHUNK * e2:256 + CHUNK * (e2 + 1)]
        return carry

    lax.fori_loop(0, r2, stage3, 0)


def _seq_dft(v):
    b, s, c = v.shape
    r2 = s // 256
    m1, m2, m3, twr, twi = _fft_constants(s)
    m1 = jnp.asarray(m1).astype(jnp.bfloat16)
    m2 = jnp.asarray(m2).astype(jnp.bfloat16)
    m3 = jnp.asarray(m3).astype(jnp.bfloat16)
    smem = pl.BlockSpec(memory_space=pltpu.SMEM)
    blk = pl.BlockSpec((1, s, FFT_LANES), lambda i, j: (i, 0, j))
    return pl.pallas_call(
        functools.partial(_fft_kernel, r2=r2),
        grid=(b, c // FFT_LANES),
        in_specs=[
            blk,
            pl.BlockSpec(m1.shape, lambda i, j: (0, 0)),
            pl.BlockSpec(m2.shape, lambda i, j: (0, 0, 0)),
            pl.BlockSpec(m3.shape, lambda i, j: (0, 0)),
            smem, smem,
        ],
        out_specs=[blk, blk],
        out_shape=[jax.ShapeDtypeStruct((b, s, c), jnp.bfloat16)] * 2,
        scratch_shapes=[pltpu.VMEM((r2, 2, 256, FFT_LANES), jnp.bfloat16)],
        compiler_params=pltpu.CompilerParams(
            dimension_semantics=("arbitrary", "arbitrary"), vmem_limit_bytes=VMEM_LIMIT),
        name="seq_dft",
    )(v, m1, m2, m3, jnp.asarray(twr), jnp.asarray(twi))


def _main_kernel(x_ref, xp_ref, xn_ref, ur_ref, ui_ref, scale_ref, shift_ref, gate_ref,
                 wc_ref, wb_ref, g_ref, wo_ref, cw_ref, cb_ref, lg_ref, lb_ref, o_ref, *, tm):
    i = pl.program_id(1)
    n_tiles = pl.num_programs(1)
    bf16 = jnp.bfloat16
    x = x_ref[0]
    s1 = 1.0 + scale_ref[0]
    sh = shift_ref[0]
    h = x * s1 + sh
    h_halo = jnp.concatenate([xp_ref[0], xn_ref[0]], axis=0) * s1 + sh
    h_ext = jnp.concatenate([h, h_halo], axis=0).astype(bf16)

    zc = jnp.dot(h_ext, wc_ref[...], preferred_element_type=jnp.float32)
    u_all = zc[:, :D_A] * zc[:, D_A:]
    u = u_all[:tm]
    u_prev = jnp.where(i > 0, u_all[tm + 7:tm + 8], 0.0)
    u_next = jnp.where(i < n_tiles - 1, u_all[tm + 8:tm + 9], 0.0)
    rows = lax.broadcasted_iota(jnp.int32, (tm, 1), 0)
    u_m1 = jnp.where(rows == 0, u_prev, pltpu.roll(u, 1, 0))
    u_p1 = jnp.where(rows == tm - 1, u_next, pltpu.roll(u, tm - 1, 0))
    conv = u_m1 * cw_ref[0:1] + u * cw_ref[1:2] + u_p1 * cw_ref[2:3] + cb_ref[...]

    zb = jnp.dot(h_ext[:tm], wb_ref[...], preferred_element_type=jnp.float32)
    bg = zb[:, :D_A]
    ga = zb[:, D_A:2 * D_A]
    gf = zb[:, 2 * D_A:]
    y_a = bg * conv * jax.nn.silu(ga)

    ur = ur_ref[0]
    ui = ui_ref[0]
    fr = []
    for g in range(N_GROUPS):
        lo, hi = g * HEAD_DIM, (g + 1) * HEAD_DIM
        lhs = jnp.concatenate([ur[:, lo:hi], ui[:, lo:hi]], axis=1)
        fr.append(jnp.dot(lhs, g_ref[g], preferred_element_type=jnp.float32))
    y_f = jnp.concatenate(fr, axis=1) * jax.nn.silu(gf)

    y = jnp.concatenate([y_a.astype(bf16), y_f.astype(bf16)], axis=1)
    o = jnp.dot(y, wo_ref[...], preferred_element_type=jnp.float32)

    r = ALPHA * x + gate_ref[0] * o
    mu = jnp.mean(r, axis=-1, keepdims=True)
    rc = r - mu
    var = jnp.mean(rc * rc, axis=-1, keepdims=True)
    o_ref[0] = rc * lax.rsqrt(var + LN_EPS) * lg_ref[...] + lb_ref[...]


def _main(x, ur, ui, scale, shift, gate, wc, wb, gfold, wo, conv_w, conv_b, ln_g, ln_b, tm):
    b, s, d = x.shape
    n8 = tm // 8
    last8 = s // 8 - 1
    tile = lambda i, j: (i, j, 0)
    vec = lambda i, j: (i, 0, 0)
    const2 = lambda i, j: (0, 0)
    return pl.pallas_call(
        functools.partial(_main_kernel, tm=tm),
        grid=(b, s // tm),
        in_specs=[
            pl.BlockSpec((1, tm, d), tile),
            pl.BlockSpec((1, 8, d), lambda i, j: (i, jnp.maximum(j * n8 - 1, 0), 0)),
            pl.BlockSpec((1, 8, d), lambda i, j: (i, jnp.minimum((j + 1) * n8, last8), 0)),
            pl.BlockSpec((1, tm, D_F), tile),
            pl.BlockSpec((1, tm, D_F), tile),
            pl.BlockSpec((1, 1, d), vec),
            pl.BlockSpec((1, 1, d), vec),
            pl.BlockSpec((1, 1, d), vec),
            pl.BlockSpec(wc.shape, const2),
            pl.BlockSpec(wb.shape, const2),
            pl.BlockSpec(gfold.shape, lambda i, j: (0, 0, 0)),
            pl.BlockSpec(wo.shape, const2),
            pl.BlockSpec(conv_w.shape, const2),
            pl.BlockSpec(conv_b.shape, const2),
            pl.BlockSpec(ln_g.shape, const2),
            pl.BlockSpec(ln_b.shape, const2),
        ],
        out_specs=pl.BlockSpec((1, tm, d), tile),
        out_shape=jax.ShapeDtypeStruct((b, s, d), jnp.float32),
        compiler_params=pltpu.CompilerParams(
            dimension_semantics=("arbitrary", "arbitrary"), vmem_limit_bytes=VMEM_LIMIT),
        name="mixer_main",
    )(x, x, x, ur, ui, scale, shift, gate, wc, wb, gfold, wo, conv_w, conv_b, ln_g, ln_b)


def kernel(x_prompt, x_sample, c_prompt, c_sample, w_ada, b_ada, w_in, conv_w, conv_b,
           w_fmix, w_out, ln_g, ln_b):
    bf16 = jnp.bfloat16
    w_ada, b_ada, w_in, conv_w, conv_b = w_ada[0], b_ada[0], w_in[0], conv_w[0], conv_b[0]
    w_fmix, w_out, ln_g, ln_b = w_fmix[0], w_out[0], ln_g[0], ln_b[0]

    w_bg, w_cg, w_va, w_ga, w_vf, w_gf = (w_in[:, k * D_A:(k + 1) * D_A].astype(bf16) for k in range(6))
    wc = jnp.concatenate([w_cg, w_va], axis=1)
    wb = jnp.concatenate([w_bg, w_ga, w_gf], axis=1)
    wo = w_out.astype(bf16)
    gfold = _fold_fmix(w_fmix)

    nb = c_prompt.shape[0]
    mod = _modulation(jnp.concatenate([c_prompt, c_sample], axis=0), w_ada, b_ada)

    outs = []
    for x, m in ((x_prompt, mod[:nb]), (x_sample, mod[nb:])):
        shift = m[:, None, :D_MODEL]
        scale = m[:, None, D_MODEL:2 * D_MODEL]
        gate = m[:, None, 2 * D_MODEL:]
        vf = _vf_proj(x, scale, shift, w_vf, tm=1024)
        ur, ui = _seq_dft(vf)
        outs.append(_main(x, ur, ui, scale, shift, gate, wc, wb, gfold, wo,
                          conv_w, conv_b.reshape(1, -1), ln_g.reshape(1, -1), ln_b.reshape(1, -1), tm=512))
    return tuple(outs)
```

```python
import functools

import numpy as np
import jax
import jax.numpy as jnp
from jax import lax
from jax.experimental import pallas as pl
from jax.experimental.pallas import tpu as pltpu

D_MODEL = 1024
HEAD_DIM = 128
N_GROUPS = 8
D_A = 1024
D_F = 1024
ALPHA = float(2.0 ** 0.25)
LN_EPS = 1e-5

CHUNK = 16
FFT_LANES = 256
VMEM_LIMIT = 56 * 1024 * 1024
FFT_UNROLL = 4


def _fft_constants(seq):
    r2 = seq // 256
    h = r2 // 16
    eye = np.eye(16)
    d = np.arange(16)
    e0 = np.arange(r2)
    c1 = np.exp(-2j * np.pi * np.outer(e0, np.arange(r2)) / r2)
    m1c = np.einsum('ab,ed->aedb', eye, c1).reshape(16 * r2, 16 * r2)
    m1 = np.concatenate([m1c.real, m1c.imag], axis=0)

    m2 = []
    for e0h in range(h):
        ph = (d[None, None, :] * d[:, None, None]) / 16.0 \
            + d[None, None, :] * (d[None, :, None] + 16 * e0h) / (16.0 * r2)
        c = np.exp(-2j * np.pi * ph)
        cf = np.einsum('ald,lm->aldm', c, eye).reshape(256, 256)
        m2.append(np.block([[cf.real, -cf.imag], [cf.imag, cf.real]]))
    m2 = np.stack(m2)

    ph = (d[None, None, :] * d[:, None, None]) / 16.0 + d[None, None, :] * d[None, :, None] / float(seq)
    c = np.exp(-2j * np.pi * ph) / np.sqrt(seq)
    cf = np.einsum('ald,lm->aldm', c, eye).reshape(256, 256)
    m3 = np.block([[cf.real, -cf.imag], [cf.imag, cf.real]])

    j = np.arange(r2)
    tw = np.exp(-2j * np.pi * (j // h)[:, None] * (16 * (j % h)[:, None] + r2 * d[None, :]) / seq)
    f32 = np.float32
    return (m1.astype(f32), m2.astype(f32), m3.astype(f32),
            tw.real.reshape(-1).astype(f32), tw.imag.reshape(-1).astype(f32))


def _channel_dft():
    n = np.arange(HEAD_DIM)
    ang = 2 * np.pi * np.outer(n, n) / HEAD_DIM
    cs = np.concatenate([np.cos(ang), np.sin(ang)], axis=0) / np.sqrt(HEAD_DIM)
    return cs.astype(np.float32)


def _mod_kernel(c_ref, w_ref, b_ref, o_ref):
    c = c_ref[...]
    o_ref[...] = jnp.dot(jax.nn.silu(c), w_ref[...], precision=lax.Precision.HIGHEST,
                         preferred_element_type=jnp.float32) + b_ref[...]


def _modulation(c_all, w_ada, b_ada):
    n = c_all.shape[0]
    return pl.pallas_call(
        _mod_kernel,
        out_shape=jax.ShapeDtypeStruct((n, 3 * D_MODEL), jnp.float32),
        compiler_params=pltpu.CompilerParams(vmem_limit_bytes=VMEM_LIMIT),
        name="adaln_mod",
    )(c_all, w_ada, b_ada.reshape(1, -1))


def _fold_kernel(cs_ref, w_ref, o_ref):
    hd = HEAD_DIM
    o_ref[...] = jnp.zeros(o_ref.shape, o_ref.dtype)
    for g in range(N_GROUPS):
        t = jnp.dot(cs_ref[...], w_ref[g], precision=lax.Precision.HIGHEST,
                    preferred_element_type=jnp.float32).astype(o_ref.dtype)
        p, k = divmod(g, 2)
        o_ref[p, k * hd:(k + 1) * hd, k * hd:(k + 1) * hd] = t[:hd]
        o_ref[p, (2 + k) * hd:(3 + k) * hd, k * hd:(k + 1) * hd] = t[hd:]


def _fold_fmix(w_fmix):
    cs = jnp.asarray(_channel_dft())
    return pl.pallas_call(
        _fold_kernel,
        out_shape=jax.ShapeDtypeStruct((N_GROUPS // 2, 4 * HEAD_DIM, 2 * HEAD_DIM), jnp.bfloat16),
        name="fold_fmix",
    )(cs, w_fmix)


def _vf_kernel(x_ref, scale_ref, shift_ref, w_ref, o_ref):
    h = x_ref[0] * (1.0 + scale_ref[0]) + shift_ref[0]
    o_ref[0] = jnp.dot(h.astype(jnp.bfloat16), w_ref[...],
                       preferred_element_type=jnp.float32).astype(o_ref.dtype)


def _vf_proj(x, scale, shift, w_vf, tm):
    b, s, d = x.shape
    return pl.pallas_call(
        _vf_kernel,
        grid=(b, s // tm),
        in_specs=[
            pl.BlockSpec((1, tm, d), lambda i, j: (i, j, 0)),
            pl.BlockSpec((1, 1, d), lambda i, j: (i, 0, 0)),
            pl.BlockSpec((1, 1, d), lambda i, j: (i, 0, 0)),
            pl.BlockSpec((d, D_F), lambda i, j: (0, 0)),
        ],
        out_specs=pl.BlockSpec((1, tm, D_F), lambda i, j: (i, j, 0)),
        out_shape=jax.ShapeDtypeStruct((b, s, D_F), jnp.bfloat16),
        compiler_params=pltpu.CompilerParams(
            dimension_semantics=("arbitrary", "arbitrary"), vmem_limit_bytes=VMEM_LIMIT),
        name="vf_proj",
    )(x, scale, shift, w_vf)


def _fft_kernel(v_ref, m1_ref, m2_ref, m3_ref, twr_ref, twi_ref, ur_ref, ui_ref, y2_ref, *, r2):
    h = r2 // 16
    k1 = 16 * r2
    bf16 = jnp.bfloat16

    def stage1(d1, carry):
        rhs = jnp.concatenate(
            [v_ref[0, pl.ds(pl.multiple_of(256 * d2 + CHUNK * d1, CHUNK), CHUNK), :] for d2 in range(r2)], axis=0)
        r = jnp.dot(m1_ref[...], rhs, preferred_element_type=jnp.float32)
        row = pl.multiple_of(d1 * k1, CHUNK)
        ur_ref[0, pl.ds(row, k1), :] = r[:k1].astype(bf16)
        ui_ref[0, pl.ds(row, k1), :] = r[k1:].astype(bf16)
        return carry

    lax.fori_loop(0, 16, stage1, 0, unroll=FFT_UNROLL)

    def stage2(j, carry):
        off = pl.multiple_of(CHUNK * j, CHUNK)
        parts = []
        for ref in (ur_ref, ui_ref):
            for d1 in range(16):
                parts.append(ref[0, pl.ds(d1 * k1 + off, CHUNK), :])
        rhs = jnp.concatenate(parts, axis=0)
        m2 = m2_ref[0] if h == 1 else m2_ref[lax.rem(j, h)]
        r = jnp.dot(m2, rhs, preferred_element_type=jnp.float32)
        for e1 in range(16):
            wr = twr_ref[j * 16 + e1]
            wi = twi_ref[j * 16 + e1]
            a = r[CHUNK * e1:CHUNK * (e1 + 1)]
            b = r[256 + CHUNK * e1:256 + CHUNK * (e1 + 1)]
            y2_ref[j, 0, CHUNK * e1:CHUNK * (e1 + 1), :] = (a * wr - b * wi).astype(bf16)
            y2_ref[j, 1, CHUNK * e1:CHUNK * (e1 + 1), :] = (a * wi + b * wr).astype(bf16)
        return carry

    lax.fori_loop(0, r2, stage2, 0, unroll=FFT_UNROLL)

    def stage3(q, carry):
        if h == 1:
            e0h, e1 = 0, q
        else:
            e0h, e1 = lax.rem(q, h), lax.div(q, h)
        off = pl.multiple_of(CHUNK * e1, CHUNK)
        parts = []
        for p in range(2):
            for d0 in range(16):
                parts.append(y2_ref[d0 * h + e0h, p, pl.ds(off, CHUNK), :])
        rhs = jnp.concatenate(parts, axis=0)
        r = jnp.dot(m3_ref[...], rhs, preferred_element_type=jnp.float32).astype(bf16)
        for e2 in range(16):
            row = pl.multiple_of(CHUNK * r2 * e2 + CHUNK * q, CHUNK)
            ur_ref[0, pl.ds(row, CHUNK), :] = r[CHUNK * e2:CHUNK * (e2 + 1)]
            ui_ref[0, pl.ds(row, CHUNK), :] = r[256 + CHUNK * e2:256 + CHUNK * (e2 + 1)]
        return carry

    lax.fori_loop(0, r2, stage3, 0, unroll=FFT_UNROLL)


def _seq_dft(v):
    b, s, c = v.shape
    r2 = s // 256
    m1, m2, m3, twr, twi = _fft_constants(s)
    m1 = jnp.asarray(m1).astype(jnp.bfloat16)
    m2 = jnp.asarray(m2).astype(jnp.bfloat16)
    m3 = jnp.asarray(m3).astype(jnp.bfloat16)
    smem = pl.BlockSpec(memory_space=pltpu.SMEM)
    blk = pl.BlockSpec((1, s, FFT_LANES), lambda i, j: (i, 0, j))
    return pl.pallas_call(
        functools.partial(_fft_kernel, r2=r2),
        grid=(b, c // FFT_LANES),
        in_specs=[
            blk,
            pl.BlockSpec(m1.shape, lambda i, j: (0, 0)),
            pl.BlockSpec(m2.shape, lambda i, j: (0, 0, 0)),
            pl.BlockSpec(m3.shape, lambda i, j: (0, 0)),
            smem, smem,
        ],
        out_specs=[blk, blk],
        out_shape=[jax.ShapeDtypeStruct((b, s, c), jnp.bfloat16)] * 2,
        scratch_shapes=[pltpu.VMEM((r2, 2, 256, FFT_LANES), jnp.bfloat16)],
        compiler_params=pltpu.CompilerParams(
            dimension_semantics=("arbitrary", "arbitrary"), vmem_limit_bytes=VMEM_LIMIT),
        name="seq_dft",
    )(v, m1, m2, m3, jnp.asarray(twr), jnp.asarray(twi))


def _main_kernel(x_ref, xp_ref, xn_ref, ur_ref, ui_ref, scale_ref, shift_ref, gate_ref,
                 wc_ref, wb_ref, g_ref, wo_ref, cw_ref, cb_ref, lg_ref, lb_ref, o_ref, r_ref, *, tm, n_tiles):
    step = pl.program_id(0)
    i = lax.rem(jnp.minimum(step, pl.num_programs(0) - 2), n_tiles)
    bf16 = jnp.bfloat16

    @pl.when(step == 0)
    def _():
        r_ref[...] = jnp.zeros(r_ref.shape, r_ref.dtype)

    r_prev = r_ref[...]
    mu = jnp.mean(r_prev, axis=-1, keepdims=True)
    rc = r_prev - mu
    var = jnp.mean(rc * rc, axis=-1, keepdims=True)
    o_ref[0] = rc * lax.rsqrt(var + LN_EPS) * lg_ref[...] + lb_ref[...]

    x = x_ref[0]
    s1 = 1.0 + scale_ref[0]
    sh = shift_ref[0]
    h = x * s1 + sh
    h_halo = jnp.concatenate([xp_ref[0], xn_ref[0]], axis=0) * s1 + sh
    h_ext = jnp.concatenate([h, h_halo], axis=0).astype(bf16)

    zc = jnp.dot(h_ext, wc_ref[...], preferred_element_type=jnp.float32)
    u_all = zc[:, :D_A] * zc[:, D_A:]
    u = u_all[:tm]
    u_prev = jnp.where(i > 0, u_all[tm + 7:tm + 8], 0.0)
    u_next = jnp.where(i < n_tiles - 1, u_all[tm + 8:tm + 9], 0.0)
    rows = lax.broadcasted_iota(jnp.int32, (tm, 1), 0)
    u_m1 = jnp.where(rows == 0, u_prev, pltpu.roll(u, 1, 0))
    u_p1 = jnp.where(rows == tm - 1, u_next, pltpu.roll(u, tm - 1, 0))
    conv = u_m1 * cw_ref[0:1] + u * cw_ref[1:2] + u_p1 * cw_ref[2:3] + cb_ref[...]

    zb = jnp.dot(h_ext[:tm], wb_ref[...], preferred_element_type=jnp.float32)
    bg = zb[:, :D_A]
    ga = zb[:, D_A:2 * D_A]
    gf = zb[:, 2 * D_A:]
    y_a = bg * conv * jax.nn.silu(ga)

    ur = ur_ref[0]
    ui = ui_ref[0]
    fr = []
    for p in range(N_GROUPS // 2):
        lo, hi = 2 * p * HEAD_DIM, 2 * (p + 1) * HEAD_DIM
        lhs = jnp.concatenate([ur[:, lo:hi], ui[:, lo:hi]], axis=1)
        fr.append(jnp.dot(lhs, g_ref[p], preferred_element_type=jnp.float32))
    y_f = jnp.concatenate(fr, axis=1) * jax.nn.silu(gf)

    y = jnp.concatenate([y_a.astype(bf16), y_f.astype(bf16)], axis=1)
    o = jnp.dot(y, wo_ref[...], preferred_element_type=jnp.float32)
    r_ref[...] = ALPHA * x + gate_ref[0] * o


def _main(x, ur, ui, scale, shift, gate, wc, wb, gfold, wo, conv_w, conv_b, ln_g, ln_b, tm):
    b, s, d = x.shape
    n_tiles = s // tm
    total = b * n_tiles
    n8 = tm // 8
    last8 = s // 8 - 1

    def cur(t):
        t = jnp.minimum(t, total - 1)
        return t // n_tiles, t % n_tiles

    def tile(t):
        i, j = cur(t)
        return i, j, 0

    def halo_prev(t):
        i, j = cur(t)
        return i, jnp.maximum(j * n8 - 1, 0), 0

    def halo_next(t):
        i, j = cur(t)
        return i, jnp.minimum((j + 1) * n8, last8), 0

    def vec(t):
        return cur(t)[0], 0, 0

    def out_tile(t):
        t = jnp.maximum(t - 1, 0)
        return t // n_tiles, t % n_tiles, 0

    const2 = lambda t: (0, 0)
    return pl.pallas_call(
        functools.partial(_main_kernel, tm=tm, n_tiles=n_tiles),
        grid=(total + 1,),
        in_specs=[
            pl.BlockSpec((1, tm, d), tile),
            pl.BlockSpec((1, 8, d), halo_prev),
            pl.BlockSpec((1, 8, d), halo_next),
            pl.BlockSpec((1, tm, D_F), tile),
            pl.BlockSpec((1, tm, D_F), tile),
            pl.BlockSpec((1, 1, d), vec),
            pl.BlockSpec((1, 1, d), vec),
            pl.BlockSpec((1, 1, d), vec),
            pl.BlockSpec(wc.shape, const2),
            pl.BlockSpec(wb.shape, const2),
            pl.BlockSpec(gfold.shape, lambda t: (0, 0, 0)),
            pl.BlockSpec(wo.shape, const2),
            pl.BlockSpec(conv_w.shape, const2),
            pl.BlockSpec(conv_b.shape, const2),
            pl.BlockSpec(ln_g.shape, const2),
            pl.BlockSpec(ln_b.shape, const2),
        ],
        out_specs=pl.BlockSpec((1, tm, d), out_tile),
        out_shape=jax.ShapeDtypeStruct((b, s, d), jnp.float32),
        scratch_shapes=[pltpu.VMEM((tm, d), jnp.float32)],
        compiler_params=pltpu.CompilerParams(
            dimension_semantics=("arbitrary",), vmem_limit_bytes=VMEM_LIMIT),
        name="mixer_main",
    )(x, x, x, ur, ui, scale, shift, gate, wc, wb, gfold, wo, conv_w, conv_b, ln_g, ln_b)


def kernel(x_prompt, x_sample, c_prompt, c_sample, w_ada, b_ada, w_in, conv_w, conv_b,
           w_fmix, w_out, ln_g, ln_b):
    bf16 = jnp.bfloat16
    w_ada, b_ada, w_in, conv_w, conv_b = w_ada[0], b_ada[0], w_in[0], conv_w[0], conv_b[0]
    w_fmix, w_out, ln_g, ln_b = w_fmix[0], w_out[0], ln_g[0], ln_b[0]

    w_bg, w_cg, w_va, w_ga, w_vf, w_gf = (w_in[:, k * D_A:(k + 1) * D_A].astype(bf16) for k in range(6))
    wc = jnp.concatenate([w_cg, w_va], axis=1)
    wb = jnp.concatenate([w_bg, w_ga, w_gf], axis=1)
    wo = w_out.astype(bf16)
    gfold = _fold_fmix(w_fmix)

    nb = c_prompt.shape[0]
    mod = _modulation(jnp.concatenate([c_prompt, c_sample], axis=0), w_ada, b_ada)

    outs = []
    for x, m in ((x_prompt, mod[:nb]), (x_sample, mod[nb:])):
        shift = m[:, None, :D_MODEL]
        scale = m[:, None, D_MODEL:2 * D_MODEL]
        gate = m[:, None, 2 * D_MODEL:]
        vf = _vf_proj(x, scale, shift, w_vf, tm=1024)
        ur, ui = _seq_dft(vf)
        outs.append(_main(x, ur, ui, scale, shift, gate, wc, wb, gfold, wo,
                          conv_w, conv_b.reshape(1, -1), ln_g.reshape(1, -1), ln_b.reshape(1, -1), tm=512))
    return tuple(outs)
```

```python
import functools

import numpy as np
import jax
import jax.numpy as jnp
from jax import lax
from jax.experimental import pallas as pl
from jax.experimental.pallas import tpu as pltpu

D_MODEL = 1024
HEAD_DIM = 128
N_GROUPS = 8
D_A = 1024
D_F = 1024
ALPHA = float(2.0 ** 0.25)
LN_EPS = 1e-5

CHUNK = 16
FFT_LANES = 256
VMEM_LIMIT = 56 * 1024 * 1024
FFT_BLOCK_CYCLES = 2048
E1_GROUP = 8


def _fft_constants(seq):
    r1 = seq // 256
    eye = np.eye(16)
    d = np.arange(16)
    pk = np.where(d[:, None] <= 8, np.cos(2 * np.pi * d[None, :] * d[:, None] / 16),
                  -np.sin(2 * np.pi * d[None, :] * (d[:, None] - 8) / 16))
    m1 = np.einsum('ab,pd->apdb', eye, pk).reshape(256, 256)

    unpack = np.zeros((16, 16), complex)
    unpack[0, 0] = unpack[8, 8] = 1
    for e0 in range(1, 8):
        unpack[e0, e0], unpack[e0, 8 + e0] = 1, 1j
        unpack[16 - e0, e0], unpack[16 - e0, 8 + e0] = 1, -1j
    dd = np.arange(r1)
    ph = dd[None, None, :] * dd[:, None, None] / r1 + dd[None, None, :] * d[None, :, None] / (16.0 * r1)
    c = np.exp(-2j * np.pi * ph)
    m2c = np.einsum('aed,ep->aedp', c, unpack).reshape(16 * r1, 16 * r1)
    blk = E1_GROUP * 16
    m2 = np.concatenate([np.concatenate([m2c.real[i:i + blk], m2c.imag[i:i + blk]], axis=0)
                         for i in range(0, 16 * r1, blk)], axis=0)

    ph = (d[None, None, :] * d[:, None, None]) / 16.0 + d[None, None, :] * d[None, :, None] / float(seq)
    c = np.exp(-2j * np.pi * ph) / np.sqrt(seq)
    cf = np.einsum('ald,lm->aldm', c, eye).reshape(256, 256)
    m3 = np.block([[cf.real, -cf.imag], [cf.imag, cf.real]])

    tw = np.exp(-2j * np.pi * np.outer(d, dd) / (16.0 * r1))
    f32 = np.float32
    return (m1.astype(f32), m2.astype(f32), m3.astype(f32),
            tw.real.reshape(-1).astype(f32), tw.imag.reshape(-1).astype(f32))


def _channel_dft():
    n = np.arange(HEAD_DIM)
    ang = 2 * np.pi * np.outer(n, n) / HEAD_DIM
    cs = np.concatenate([np.cos(ang), np.sin(ang)], axis=0) / np.sqrt(HEAD_DIM)
    return cs.astype(np.float32)


def _mod_kernel(c_ref, w_ref, b_ref, o_ref):
    c = c_ref[...]
    o_ref[...] = jnp.dot(jax.nn.silu(c), w_ref[...], precision=lax.Precision.HIGHEST,
                         preferred_element_type=jnp.float32) + b_ref[...]


def _modulation(c_all, w_ada, b_ada):
    n = c_all.shape[0]
    return pl.pallas_call(
        _mod_kernel,
        out_shape=jax.ShapeDtypeStruct((n, 3 * D_MODEL), jnp.float32),
        compiler_params=pltpu.CompilerParams(vmem_limit_bytes=VMEM_LIMIT),
        name="adaln_mod",
    )(c_all, w_ada, b_ada.reshape(1, -1))


def _fold_kernel(cs_ref, w_ref, o_ref):
    hd = HEAD_DIM
    o_ref[...] = jnp.zeros(o_ref.shape, o_ref.dtype)
    for g in range(N_GROUPS):
        t = jnp.dot(cs_ref[...], w_ref[g], precision=lax.Precision.HIGHEST,
                    preferred_element_type=jnp.float32).astype(o_ref.dtype)
        p, k = divmod(g, 2)
        o_ref[p, k * hd:(k + 1) * hd, k * hd:(k + 1) * hd] = t[:hd]
        o_ref[p, (2 + k) * hd:(3 + k) * hd, k * hd:(k + 1) * hd] = t[hd:]


def _fold_fmix(w_fmix):
    cs = jnp.asarray(_channel_dft())
    return pl.pallas_call(
        _fold_kernel,
        out_shape=jax.ShapeDtypeStruct((N_GROUPS // 2, 4 * HEAD_DIM, 2 * HEAD_DIM), jnp.bfloat16),
        name="fold_fmix",
    )(cs, w_fmix)


def _vf_kernel(x_ref, scale_ref, shift_ref, w_ref, o_ref):
    h = x_ref[0] * (1.0 + scale_ref[0]) + shift_ref[0]
    o_ref[0] = jnp.dot(h.astype(jnp.bfloat16), w_ref[...],
                       preferred_element_type=jnp.float32).astype(o_ref.dtype)


def _vf_proj(x, scale, shift, w_vf, tm):
    b, s, d = x.shape
    return pl.pallas_call(
        _vf_kernel,
        grid=(b, s // tm),
        in_specs=[
            pl.BlockSpec((1, tm, d), lambda i, j: (i, j, 0)),
            pl.BlockSpec((1, 1, d), lambda i, j: (i, 0, 0)),
            pl.BlockSpec((1, 1, d), lambda i, j: (i, 0, 0)),
            pl.BlockSpec((d, D_F), lambda i, j: (0, 0)),
        ],
        out_specs=pl.BlockSpec((1, tm, D_F), lambda i, j: (i, j, 0)),
        out_shape=jax.ShapeDtypeStruct((b, s, D_F), jnp.bfloat16),
        compiler_params=pltpu.CompilerParams(
            dimension_semantics=("arbitrary", "arbitrary"), vmem_limit_bytes=VMEM_LIMIT),
        name="vf_proj",
    )(x, scale, shift, w_vf)


def _unroll(trips, mxu_cycles):
    return max(1, min(trips, FFT_BLOCK_CYCLES // mxu_cycles))


def _fft_kernel(v_ref, m1_ref, m2_ref, m3_ref, twr_ref, twi_ref, ur_ref, ui_ref, y2_ref, *, r1):
    k2 = CHUNK * r1
    bf16 = jnp.bfloat16

    def stage1(d1, carry):
        rhs = jnp.concatenate(
            [v_ref[0, pl.ds(pl.multiple_of(k2 * d2 + CHUNK * d1, CHUNK), CHUNK), :] for d2 in range(16)], axis=0)
        r = jnp.dot(m1_ref[...], rhs, preferred_element_type=jnp.float32)
        ur_ref[0, pl.ds(pl.multiple_of(256 * d1, 256), 256), :] = r.astype(bf16)
        return carry

    lax.fori_loop(0, r1, stage1, 0, unroll=_unroll(r1, 64))

    def stage2(d0, carry):
        off = pl.multiple_of(CHUNK * d0, CHUNK)
        rhs = jnp.concatenate([ur_ref[0, pl.ds(256 * d1 + off, CHUNK), :] for d1 in range(r1)], axis=0)
        blk = 2 * E1_GROUP * CHUNK
        for g in range(r1 // E1_GROUP):
            r = jnp.dot(m2_ref[g * blk:(g + 1) * blk, :], rhs, preferred_element_type=jnp.float32)
            for j in range(E1_GROUP):
                e1 = g * E1_GROUP + j
                wr = twr_ref[d0 * r1 + e1]
                wi = twi_ref[d0 * r1 + e1]
                a = r[CHUNK * j:CHUNK * (j + 1)]
                b = r[blk // 2 + CHUNK * j:blk // 2 + CHUNK * (j + 1)]
                y2_ref[d0, 0, CHUNK * e1:CHUNK * (e1 + 1), :] = (a * wr - b * wi).astype(bf16)
                y2_ref[d0, 1, CHUNK * e1:CHUNK * (e1 + 1), :] = (a * wi + b * wr).astype(bf16)
        return carry

    lax.fori_loop(0, 16, stage2, 0, unroll=_unroll(16, r1 * r1 // 2))

    def stage3(e1, carry):
        off = pl.multiple_of(CHUNK * e1, CHUNK)
        rhs = jnp.concatenate(
            [y2_ref[d0, p, pl.ds(off, CHUNK), :] for p in range(2) for d0 in range(16)], axis=0)
        r = jnp.dot(m3_ref[...], rhs, preferred_element_type=jnp.float32).astype(bf16)
        for e2 in range(16):
            row = pl.multiple_of(k2 * e2 + off, CHUNK)
            ur_ref[0, pl.ds(row, CHUNK), :] = r[CHUNK * e2:CHUNK * (e2 + 1)]
            ui_ref[0, pl.ds(row, CHUNK), :] = r[256 + CHUNK * e2:256 + CHUNK * (e2 + 1)]
        return carry

    lax.fori_loop(0, r1, stage3, 0, unroll=_unroll(r1, 256))


def _seq_dft(v):
    b, s, c = v.shape
    r1 = s // 256
    m1, m2, m3, twr, twi = _fft_constants(s)
    m1 = jnp.asarray(m1).astype(jnp.bfloat16)
    m2 = jnp.asarray(m2).astype(jnp.bfloat16)
    m3 = jnp.asarray(m3).astype(jnp.bfloat16)
    smem = pl.BlockSpec(memory_space=pltpu.SMEM)
    blk = pl.BlockSpec((1, s, FFT_LANES), lambda i, j: (i, 0, j))
    const2 = lambda i, j: (0, 0)
    return pl.pallas_call(
        functools.partial(_fft_kernel, r1=r1),
        grid=(b, c // FFT_LANES),
        in_specs=[
            blk,
            pl.BlockSpec(m1.shape, const2),
            pl.BlockSpec(m2.shape, const2),
            pl.BlockSpec(m3.shape, const2),
            smem, smem,
        ],
        out_specs=[blk, blk],
        out_shape=[jax.ShapeDtypeStruct((b, s, c), jnp.bfloat16)] * 2,
        scratch_shapes=[pltpu.VMEM((16, 2, CHUNK * r1, FFT_LANES), jnp.bfloat16)],
        compiler_params=pltpu.CompilerParams(
            dimension_semantics=("arbitrary", "arbitrary"), vmem_limit_bytes=VMEM_LIMIT),
        name="seq_dft",
    )(v, m1, m2, m3, jnp.asarray(twr), jnp.asarray(twi))


def _main_kernel(x_ref, xp_ref, xn_ref, ur_ref, ui_ref, scale_ref, shift_ref, gate_ref,
                 wc_ref, wb_ref, g_ref, wo_ref, cw_ref, cb_ref, lg_ref, lb_ref, o_ref, *, tm):
    i = pl.program_id(1)
    n_tiles = pl.num_programs(1)
    bf16 = jnp.bfloat16
    x = x_ref[0]
    s1 = 1.0 + scale_ref[0]
    sh = shift_ref[0]
    h = x * s1 + sh
    h_halo = jnp.concatenate([xp_ref[0], xn_ref[0]], axis=0) * s1 + sh
    h_ext = jnp.concatenate([h, h_halo], axis=0).astype(bf16)

    zc = jnp.dot(h_ext, wc_ref[...], preferred_element_type=jnp.float32)
    u_all = zc[:, :D_A] * zc[:, D_A:]
    u = u_all[:tm]
    u_prev = jnp.where(i > 0, u_all[tm + 7:tm + 8], 0.0)
    u_next = jnp.where(i < n_tiles - 1, u_all[tm + 8:tm + 9], 0.0)
    rows = lax.broadcasted_iota(jnp.int32, (tm, 1), 0)
    u_m1 = jnp.where(rows == 0, u_prev, pltpu.roll(u, 1, 0))
    u_p1 = jnp.where(rows == tm - 1, u_next, pltpu.roll(u, tm - 1, 0))
    conv = u_m1 * cw_ref[0:1] + u * cw_ref[1:2] + u_p1 * cw_ref[2:3] + cb_ref[...]

    zb = jnp.dot(h_ext[:tm], wb_ref[...], preferred_element_type=jnp.float32)
    bg = zb[:, :D_A]
    ga = zb[:, D_A:2 * D_A]
    gf = zb[:, 2 * D_A:]
    y_a = bg * conv * jax.nn.silu(ga)

    ur = ur_ref[0]
    ui = ui_ref[0]
    fr = []
    for p in range(N_GROUPS // 2):
        lo, hi = 2 * p * HEAD_DIM, 2 * (p + 1) * HEAD_DIM
        lhs = jnp.concatenate([ur[:, lo:hi], ui[:, lo:hi]], axis=1)
        fr.append(jnp.dot(lhs, g_ref[p], preferred_element_type=jnp.float32))
    y_f = jnp.concatenate(fr, axis=1) * jax.nn.silu(gf)

    y = jnp.concatenate([y_a.astype(bf16), y_f.astype(bf16)], axis=1)
    o = jnp.dot(y, wo_ref[...], preferred_element_type=jnp.float32)

    r = ALPHA * x + gate_ref[0] * o
    mu = jnp.mean(r, axis=-1, keepdims=True)
    rc = r - mu
    var = jnp.mean(rc * rc, axis=-1, keepdims=True)
    o_ref[0] = rc * lax.rsqrt(var + LN_EPS) * lg_ref[...] + lb_ref[...]


def _main(x, ur, ui, scale, shift, gate, wc, wb, gfold, wo, conv_w, conv_b, ln_g, ln_b, tm):
    b, s, d = x.shape
    n8 = tm // 8
    last8 = s // 8 - 1
    tile = lambda i, j: (i, j, 0)
    vec = lambda i, j: (i, 0, 0)
    const2 = lambda i, j: (0, 0)
    return pl.pallas_call(
        functools.partial(_main_kernel, tm=tm),
        grid=(b, s // tm),
        in_specs=[
            pl.BlockSpec((1, tm, d), tile),
            pl.BlockSpec((1, 8, d), lambda i, j: (i, jnp.maximum(j * n8 - 1, 0), 0)),
            pl.BlockSpec((1, 8, d), lambda i, j: (i, jnp.minimum((j + 1) * n8, last8), 0)),
            pl.BlockSpec((1, tm, D_F), tile),
            pl.BlockSpec((1, tm, D_F), tile),
            pl.BlockSpec((1, 1, d), vec),
            pl.BlockSpec((1, 1, d), vec),
            pl.BlockSpec((1, 1, d), vec),
            pl.BlockSpec(wc.shape, const2),
            pl.BlockSpec(wb.shape, const2),
            pl.BlockSpec(gfold.shape, lambda i, j: (0, 0, 0)),
            pl.BlockSpec(wo.shape, const2),
            pl.BlockSpec(conv_w.shape, const2),
            pl.BlockSpec(conv_b.shape, const2),
            pl.BlockSpec(ln_g.shape, const2),
            pl.BlockSpec(ln_b.shape, const2),
        ],
        out_specs=pl.BlockSpec((1, tm, d), tile),
        out_shape=jax.ShapeDtypeStruct((b, s, d), jnp.float32),
        compiler_params=pltpu.CompilerParams(
            dimension_semantics=("arbitrary", "arbitrary"), vmem_limit_bytes=VMEM_LIMIT),
        name="mixer_main",
    )(x, x, x, ur, ui, scale, shift, gate, wc, wb, gfold, wo, conv_w, conv_b, ln_g, ln_b)


def kernel(x_prompt, x_sample, c_prompt, c_sample, w_ada, b_ada, w_in, conv_w, conv_b,
           w_fmix, w_out, ln_g, ln_b):
    bf16 = jnp.bfloat16
    w_ada, b_ada, w_in, conv_w, conv_b = w_ada[0], b_ada[0], w_in[0], conv_w[0], conv_b[0]
    w_fmix, w_out, ln_g, ln_b = w_fmix[0], w_out[0], ln_g[0], ln_b[0]

    w_bg, w_cg, w_va, w_ga, w_vf, w_gf = (w_in[:, k * D_A:(k + 1) * D_A].astype(bf16) for k in range(6))
    wc = jnp.concatenate([w_cg, w_va], axis=1)
    wb = jnp.concatenate([w_bg, w_ga, w_gf], axis=1)
    wo = w_out.astype(bf16)
    gfold = _fold_fmix(w_fmix)

    nb = c_prompt.shape[0]
    mod = _modulation(jnp.concatenate([c_prompt, c_sample], axis=0), w_ada, b_ada)

    outs = []
    for x, m in ((x_prompt, mod[:nb]), (x_sample, mod[nb:])):
        shift = m[:, None, :D_MODEL]
        scale = m[:, None, D_MODEL:2 * D_MODEL]
        gate = m[:, None, 2 * D_MODEL:]
        vf = _vf_proj(x, scale, shift, w_vf, tm=1024)
        ur, ui = _seq_dft(vf)
        outs.append(_main(x, ur, ui, scale, shift, gate, wc, wb, gfold, wo,
                          conv_w, conv_b.reshape(1, -1), ln_g.reshape(1, -1), ln_b.reshape(1, -1), tm=512))
    return tuple(outs)
```

```python
import functools

import numpy as np
import jax
import jax.numpy as jnp
from jax import lax
from jax.experimental import pallas as pl
from jax.experimental.pallas import tpu as pltpu

D_MODEL = 1024
HEAD_DIM = 128
N_GROUPS = 8
D_A = 1024
D_F = 1024
ALPHA = float(2.0 ** 0.25)
LN_EPS = 1e-5

COL_BG, COL_CG, COL_VA, COL_GA, COL_VF, COL_GF = range(6)
MOD_SHIFT, MOD_SCALE, MOD_GATE = range(3)
VF_ROWS = 2048
MAIN_ROWS = 1024

CHUNK = 16
FFT_LANES = 256
VMEM_LIMIT = 56 * 1024 * 1024
FFT_BLOCK_CYCLES = 2048
E1_GROUP = 8


def _fft_constants(seq):
    r1 = seq // 256
    eye = np.eye(16)
    d = np.arange(16)
    pk = np.where(d[:, None] <= 8, np.cos(2 * np.pi * d[None, :] * d[:, None] / 16),
                  -np.sin(2 * np.pi * d[None, :] * (d[:, None] - 8) / 16))
    m1 = np.einsum('ab,pd->apdb', eye, pk).reshape(256, 256)

    unpack = np.zeros((16, 16), complex)
    unpack[0, 0] = unpack[8, 8] = 1
    for e0 in range(1, 8):
        unpack[e0, e0], unpack[e0, 8 + e0] = 1, 1j
        unpack[16 - e0, e0], unpack[16 - e0, 8 + e0] = 1, -1j
    dd = np.arange(r1)
    ph = dd[None, None, :] * dd[:, None, None] / r1 + dd[None, None, :] * d[None, :, None] / (16.0 * r1)
    c = np.exp(-2j * np.pi * ph)
    m2c = np.einsum('aed,ep->aedp', c, unpack).reshape(16 * r1, 16 * r1)
    blk = E1_GROUP * 16
    m2 = np.concatenate([np.concatenate([m2c.real[i:i + blk], m2c.imag[i:i + blk]], axis=0)
                         for i in range(0, 16 * r1, blk)], axis=0)

    ph = (d[None, None, :] * d[:, None, None]) / 16.0 + d[None, None, :] * d[None, :, None] / float(seq)
    c = np.exp(-2j * np.pi * ph) / np.sqrt(seq)
    cf = np.einsum('ald,lm->aldm', c, eye).reshape(256, 256)
    m3 = np.block([[cf.real, -cf.imag], [cf.imag, cf.real]])

    tw = np.exp(-2j * np.pi * np.outer(d, dd) / (16.0 * r1))
    f32 = np.float32
    return (m1.astype(f32), m2.astype(f32), m3.astype(f32),
            tw.real.reshape(-1).astype(f32), tw.imag.reshape(-1).astype(f32))


def _channel_dft():
    n = np.arange(HEAD_DIM)
    ang = 2 * np.pi * np.outer(n, n) / HEAD_DIM
    cs = np.concatenate([np.cos(ang), np.sin(ang)], axis=0) / np.sqrt(HEAD_DIM)
    return cs.astype(np.float32)


def _mod_kernel(c_ref, w_ref, b_ref, o_ref):
    c = c_ref[...]
    o_ref[...] = jnp.dot(jax.nn.silu(c), w_ref[...], precision=lax.Precision.HIGHEST,
                         preferred_element_type=jnp.float32) + b_ref[...]


def _modulation(c_all, w_ada, b_ada):
    n, d = c_all.shape
    tn = 512
    return pl.pallas_call(
        _mod_kernel,
        grid=(3 * D_MODEL // tn,),
        in_specs=[
            pl.BlockSpec((n, d), lambda j: (0, 0)),
            pl.BlockSpec((d, tn), lambda j: (0, j)),
            pl.BlockSpec((1, tn), lambda j: (0, j)),
        ],
        out_specs=pl.BlockSpec((n, tn), lambda j: (0, j)),
        out_shape=jax.ShapeDtypeStruct((n, 3 * D_MODEL), jnp.float32),
        compiler_params=pltpu.CompilerParams(dimension_semantics=("arbitrary",)),
        name="adaln_mod",
    )(c_all, w_ada, b_ada.reshape(1, -1))


def _fold_kernel(cs_ref, w_ref, o_ref):
    hd = HEAD_DIM
    o_ref[...] = jnp.zeros(o_ref.shape, o_ref.dtype)
    for g in range(N_GROUPS):
        t = jnp.dot(cs_ref[...], w_ref[g], precision=lax.Precision.HIGHEST,
                    preferred_element_type=jnp.float32).astype(o_ref.dtype)
        p, k = divmod(g, 2)
        o_ref[p, k * hd:(k + 1) * hd, k * hd:(k + 1) * hd] = t[:hd]
        o_ref[p, (2 + k) * hd:(3 + k) * hd, k * hd:(k + 1) * hd] = t[hd:]


def _fold_fmix(w_fmix):
    cs = jnp.asarray(_channel_dft())
    return pl.pallas_call(
        _fold_kernel,
        out_shape=jax.ShapeDtypeStruct((N_GROUPS // 2, 4 * HEAD_DIM, 2 * HEAD_DIM), jnp.bfloat16),
        name="fold_fmix",
    )(cs, w_fmix)


def _vf_kernel(x_ref, scale_ref, shift_ref, w_ref, o_ref):
    h = x_ref[0] * (1.0 + scale_ref[0]) + shift_ref[0]
    o_ref[0] = jnp.dot(h.astype(jnp.bfloat16), w_ref[...],
                       preferred_element_type=jnp.float32).astype(o_ref.dtype)


def _vf_proj(x, mod, row0, w_in, tm):
    b, s, d = x.shape
    return pl.pallas_call(
        _vf_kernel,
        grid=(b, s // tm),
        in_specs=[
            pl.BlockSpec((1, tm, d), lambda i, j: (i, j, 0)),
            pl.BlockSpec((1, 1, d), lambda i, j: (row0 + i, 0, MOD_SCALE)),
            pl.BlockSpec((1, 1, d), lambda i, j: (row0 + i, 0, MOD_SHIFT)),
            pl.BlockSpec((d, D_F), lambda i, j: (0, COL_VF)),
        ],
        out_specs=pl.BlockSpec((1, tm, D_F), lambda i, j: (i, j, 0)),
        out_shape=jax.ShapeDtypeStruct((b, s, D_F), jnp.bfloat16),
        compiler_params=pltpu.CompilerParams(
            dimension_semantics=("arbitrary", "arbitrary"), vmem_limit_bytes=VMEM_LIMIT),
        name="vf_proj",
    )(x, mod, mod, w_in)


def _unroll(trips, mxu_cycles):
    return max(1, min(trips, FFT_BLOCK_CYCLES // mxu_cycles))


def _fft_kernel(v_ref, m1_ref, m2_ref, m3_ref, twr_ref, twi_ref, ur_ref, ui_ref, y2_ref, *, r1):
    k2 = CHUNK * r1
    bf16 = jnp.bfloat16

    def stage1(d1, carry):
        rhs = jnp.concatenate(
            [v_ref[0, pl.ds(pl.multiple_of(k2 * d2 + CHUNK * d1, CHUNK), CHUNK), :] for d2 in range(16)], axis=0)
        r = jnp.dot(m1_ref[...], rhs, preferred_element_type=jnp.float32)
        ur_ref[0, pl.ds(pl.multiple_of(256 * d1, 256), 256), :] = r.astype(bf16)
        return carry

    lax.fori_loop(0, r1, stage1, 0, unroll=_unroll(r1, 64))

    def stage2(d0, carry):
        off = pl.multiple_of(CHUNK * d0, CHUNK)
        rhs = jnp.concatenate([ur_ref[0, pl.ds(256 * d1 + off, CHUNK), :] for d1 in range(r1)], axis=0)
        blk = 2 * E1_GROUP * CHUNK
        for g in range(r1 // E1_GROUP):
            r = jnp.dot(m2_ref[g * blk:(g + 1) * blk, :], rhs, preferred_element_type=jnp.float32)
            for j in range(E1_GROUP):
                e1 = g * E1_GROUP + j
                wr = twr_ref[d0 * r1 + e1]
                wi = twi_ref[d0 * r1 + e1]
                a = r[CHUNK * j:CHUNK * (j + 1)]
                b = r[blk // 2 + CHUNK * j:blk // 2 + CHUNK * (j + 1)]
                y2_ref[d0, 0, CHUNK * e1:CHUNK * (e1 + 1), :] = (a * wr - b * wi).astype(bf16)
                y2_ref[d0, 1, CHUNK * e1:CHUNK * (e1 + 1), :] = (a * wi + b * wr).astype(bf16)
        return carry

    lax.fori_loop(0, 16, stage2, 0, unroll=_unroll(16, r1 * r1 // 2))

    def stage3(e1, carry):
        off = pl.multiple_of(CHUNK * e1, CHUNK)
        rhs = jnp.concatenate(
            [y2_ref[d0, p, pl.ds(off, CHUNK), :] for p in range(2) for d0 in range(16)], axis=0)
        r = jnp.dot(m3_ref[...], rhs, preferred_element_type=jnp.float32).astype(bf16)
        for e2 in range(16):
            row = pl.multiple_of(k2 * e2 + off, CHUNK)
            ur_ref[0, pl.ds(row, CHUNK), :] = r[CHUNK * e2:CHUNK * (e2 + 1)]
            ui_ref[0, pl.ds(row, CHUNK), :] = r[256 + CHUNK * e2:256 + CHUNK * (e2 + 1)]
        return carry

    lax.fori_loop(0, r1, stage3, 0, unroll=_unroll(r1, 256))


def _seq_dft(v):
    b, s, c = v.shape
    r1 = s // 256
    m1, m2, m3, twr, twi = _fft_constants(s)
    m1 = jnp.asarray(m1).astype(jnp.bfloat16)
    m2 = jnp.asarray(m2).astype(jnp.bfloat16)
    m3 = jnp.asarray(m3).astype(jnp.bfloat16)
    smem = pl.BlockSpec(memory_space=pltpu.SMEM)
    blk = pl.BlockSpec((1, s, FFT_LANES), lambda i, j: (i, 0, j))
    const2 = lambda i, j: (0, 0)
    return pl.pallas_call(
        functools.partial(_fft_kernel, r1=r1),
        grid=(b, c // FFT_LANES),
        in_specs=[
            blk,
            pl.BlockSpec(m1.shape, const2),
            pl.BlockSpec(m2.shape, const2),
            pl.BlockSpec(m3.shape, const2),
            smem, smem,
        ],
        out_specs=[blk, blk],
        out_shape=[jax.ShapeDtypeStruct((b, s, c), jnp.bfloat16)] * 2,
        scratch_shapes=[pltpu.VMEM((16, 2, CHUNK * r1, FFT_LANES), jnp.bfloat16)],
        compiler_params=pltpu.CompilerParams(
            dimension_semantics=("arbitrary", "arbitrary"), vmem_limit_bytes=VMEM_LIMIT),
        name="seq_dft",
    )(v, m1, m2, m3, jnp.asarray(twr), jnp.asarray(twi))


def _main_kernel(x_ref, xp_ref, xn_ref, ur_ref, ui_ref, scale_ref, shift_ref, gate_ref,
                 w_ref, g_ref, wo_ref, cw_ref, cb_ref, lg_ref, lb_ref, o_ref, *, tm):
    i = pl.program_id(1)
    n_tiles = pl.num_programs(1)
    bf16 = jnp.bfloat16
    x = x_ref[0]
    s1 = 1.0 + scale_ref[0]
    sh = shift_ref[0]
    h = x * s1 + sh
    h_halo = jnp.concatenate([xp_ref[0], xn_ref[0]], axis=0) * s1 + sh
    h_ext = jnp.concatenate([h, h_halo], axis=0).astype(bf16)

    zc = jnp.dot(h_ext, w_ref[:, COL_CG * D_A:(COL_VA + 1) * D_A],
                 preferred_element_type=jnp.float32)
    u_all = zc[:, :D_A] * zc[:, D_A:]
    u = u_all[:tm]
    u_prev = jnp.where(i > 0, u_all[tm + 7:tm + 8], 0.0)
    u_next = jnp.where(i < n_tiles - 1, u_all[tm + 8:tm + 9], 0.0)
    rows = lax.broadcasted_iota(jnp.int32, (tm, 1), 0)
    u_m1 = jnp.where(rows == 0, u_prev, pltpu.roll(u, 1, 0))
    u_p1 = jnp.where(rows == tm - 1, u_next, pltpu.roll(u, tm - 1, 0))
    conv = u_m1 * cw_ref[0:1] + u * cw_ref[1:2] + u_p1 * cw_ref[2:3] + cb_ref[...]

    hb = h_ext[:tm]
    bg, ga, gf = (jnp.dot(hb, w_ref[:, k * D_A:(k + 1) * D_A], preferred_element_type=jnp.float32)
                  for k in (COL_BG, COL_GA, COL_GF))
    y_a = bg * conv * jax.nn.silu(ga)

    ur = ur_ref[0]
    ui = ui_ref[0]
    fr = []
    for p in range(N_GROUPS // 2):
        lo, hi = 2 * p * HEAD_DIM, 2 * (p + 1) * HEAD_DIM
        lhs = jnp.concatenate([ur[:, lo:hi], ui[:, lo:hi]], axis=1)
        fr.append(jnp.dot(lhs, g_ref[p], preferred_element_type=jnp.float32))
    y_f = jnp.concatenate(fr, axis=1) * jax.nn.silu(gf)

    y = jnp.concatenate([y_a.astype(bf16), y_f.astype(bf16)], axis=1)
    o = jnp.dot(y, wo_ref[...], preferred_element_type=jnp.float32)

    r = ALPHA * x + gate_ref[0] * o
    mu = jnp.mean(r, axis=-1, keepdims=True)
    rc = r - mu
    var = jnp.mean(rc * rc, axis=-1, keepdims=True)
    o_ref[0] = rc * lax.rsqrt(var + LN_EPS) * lg_ref[...] + lb_ref[...]


def _main(x, ur, ui, mod, row0, w_in, gfold, wo, conv_w, conv_b, ln_g, ln_b, tm):
    b, s, d = x.shape
    n8 = tm // 8
    last8 = s // 8 - 1
    tile = lambda i, j: (i, j, 0)
    const2 = lambda i, j: (0, 0)
    mod_spec = lambda k: pl.BlockSpec((1, 1, d), lambda i, j: (row0 + i, 0, k))
    return pl.pallas_call(
        functools.partial(_main_kernel, tm=tm),
        grid=(b, s // tm),
        in_specs=[
            pl.BlockSpec((1, tm, d), tile),
            pl.BlockSpec((1, 8, d), lambda i, j: (i, jnp.maximum(j * n8 - 1, 0), 0)),
            pl.BlockSpec((1, 8, d), lambda i, j: (i, jnp.minimum((j + 1) * n8, last8), 0)),
            pl.BlockSpec((1, tm, D_F), tile),
            pl.BlockSpec((1, tm, D_F), tile),
            mod_spec(MOD_SCALE),
            mod_spec(MOD_SHIFT),
            mod_spec(MOD_GATE),
            pl.BlockSpec(w_in.shape, const2),
            pl.BlockSpec(gfold.shape, lambda i, j: (0, 0, 0)),
            pl.BlockSpec(wo.shape, const2),
            pl.BlockSpec(conv_w.shape, const2),
            pl.BlockSpec(conv_b.shape, const2),
            pl.BlockSpec(ln_g.shape, const2),
            pl.BlockSpec(ln_b.shape, const2),
        ],
        out_specs=pl.BlockSpec((1, tm, d), tile),
        out_shape=jax.ShapeDtypeStruct((b, s, d), jnp.float32),
        compiler_params=pltpu.CompilerParams(
            dimension_semantics=("arbitrary", "arbitrary"), vmem_limit_bytes=VMEM_LIMIT),
        name="mixer_main",
    )(x, x, x, ur, ui, mod, mod, mod, w_in, gfold, wo, conv_w, conv_b, ln_g, ln_b)


def kernel(x_prompt, x_sample, c_prompt, c_sample, w_ada, b_ada, w_in, conv_w, conv_b,
           w_fmix, w_out, ln_g, ln_b):
    bf16 = jnp.bfloat16
    w_ada, b_ada, w_in, conv_w, conv_b = w_ada[0], b_ada[0], w_in[0], conv_w[0], conv_b[0]
    w_fmix, w_out, ln_g, ln_b = w_fmix[0], w_out[0], ln_g[0], ln_b[0]

    w_in = w_in.astype(bf16)
    wo = w_out.astype(bf16)
    gfold = _fold_fmix(w_fmix)

    nb = c_prompt.shape[0]
    mod = _modulation(jnp.concatenate([c_prompt, c_sample], axis=0), w_ada, b_ada)
    mod = mod.reshape(mod.shape[0], 1, 3 * D_MODEL)

    outs = []
    for x, row0 in ((x_prompt, 0), (x_sample, nb)):
        vf = _vf_proj(x, mod, row0, w_in, tm=VF_ROWS)
        ur, ui = _seq_dft(vf)
        outs.append(_main(x, ur, ui, mod, row0, w_in, gfold, wo, conv_w, conv_b.reshape(1, -1),
                          ln_g.reshape(1, -1), ln_b.reshape(1, -1), tm=MAIN_ROWS))
    return tuple(outs)
```

```python
import functools

import numpy as np
import jax
import jax.numpy as jnp
from jax import lax
from jax.experimental import pallas as pl
from jax.experimental.pallas import tpu as pltpu

D_MODEL = 1024
HEAD_DIM = 128
N_GROUPS = 8
D_A = 1024
D_F = 1024
ALPHA = float(2.0 ** 0.25)
LN_EPS = 1e-5

COL_BG, COL_CG, COL_VA, COL_GA, COL_VF, COL_GF = range(6)
MOD_SHIFT, MOD_SCALE, MOD_GATE = range(3)
VF_ROWS = 2048
MAIN_ROWS = 1024
OUT_ROWS = 256
COL_BLOCK = 256

CHUNK = 16
FFT_LANES = 256
VMEM_LIMIT = 56 * 1024 * 1024
FFT_BLOCK_CYCLES = 2048
E1_GROUP = 8


def _fft_constants(seq):
    r1 = seq // 256
    eye = np.eye(16)
    d = np.arange(16)
    pk = np.where(d[:, None] <= 8, np.cos(2 * np.pi * d[None, :] * d[:, None] / 16),
                  -np.sin(2 * np.pi * d[None, :] * (d[:, None] - 8) / 16))
    m1 = np.einsum('ab,pd->apdb', eye, pk).reshape(256, 256)

    unpack = np.zeros((16, 16), complex)
    unpack[0, 0] = unpack[8, 8] = 1
    for e0 in range(1, 8):
        unpack[e0, e0], unpack[e0, 8 + e0] = 1, 1j
        unpack[16 - e0, e0], unpack[16 - e0, 8 + e0] = 1, -1j
    dd = np.arange(r1)
    ph = dd[None, None, :] * dd[:, None, None] / r1 + dd[None, None, :] * d[None, :, None] / (16.0 * r1)
    c = np.exp(-2j * np.pi * ph)
    m2c = np.einsum('aed,ep->aedp', c, unpack).reshape(16 * r1, 16 * r1)
    blk = E1_GROUP * 16
    m2 = np.concatenate([np.concatenate([m2c.real[i:i + blk], m2c.imag[i:i + blk]], axis=0)
                         for i in range(0, 16 * r1, blk)], axis=0)

    ph = (d[None, None, :] * d[:, None, None]) / 16.0 + d[None, None, :] * d[None, :, None] / float(seq)
    c = np.exp(-2j * np.pi * ph) / np.sqrt(seq)
    cf = np.einsum('ald,lm->aldm', c, eye).reshape(256, 256)
    m3 = np.block([[cf.real, -cf.imag], [cf.imag, cf.real]])

    tw = np.exp(-2j * np.pi * np.outer(d, dd) / (16.0 * r1))
    f32 = np.float32
    return (m1.astype(f32), m2.astype(f32), m3.astype(f32),
            tw.real.reshape(-1).astype(f32), tw.imag.reshape(-1).astype(f32))


def _channel_dft():
    n = np.arange(HEAD_DIM)
    ang = 2 * np.pi * np.outer(n, n) / HEAD_DIM
    cs = np.concatenate([np.cos(ang), np.sin(ang)], axis=0) / np.sqrt(HEAD_DIM)
    return cs.astype(np.float32)


def _mod_kernel(c_ref, w_ref, b_ref, o_ref):
    c = c_ref[...]
    o_ref[...] = jnp.dot(jax.nn.silu(c), w_ref[...], precision=lax.Precision.HIGHEST,
                         preferred_element_type=jnp.float32) + b_ref[...]


def _modulation(c_all, w_ada, b_ada):
    n, d = c_all.shape
    tn = 512
    return pl.pallas_call(
        _mod_kernel,
        grid=(3 * D_MODEL // tn,),
        in_specs=[
            pl.BlockSpec((n, d), lambda j: (0, 0)),
            pl.BlockSpec((d, tn), lambda j: (0, j)),
            pl.BlockSpec((1, tn), lambda j: (0, j)),
        ],
        out_specs=pl.BlockSpec((n, tn), lambda j: (0, j)),
        out_shape=jax.ShapeDtypeStruct((n, 3 * D_MODEL), jnp.float32),
        compiler_params=pltpu.CompilerParams(dimension_semantics=("arbitrary",)),
        name="adaln_mod",
    )(c_all, w_ada, b_ada.reshape(1, -1))


def _fold_kernel(cs_ref, w_ref, o_ref):
    hd = HEAD_DIM
    o_ref[...] = jnp.zeros(o_ref.shape, o_ref.dtype)
    for g in range(N_GROUPS):
        t = jnp.dot(cs_ref[...], w_ref[g], precision=lax.Precision.HIGHEST,
                    preferred_element_type=jnp.float32).astype(o_ref.dtype)
        p, k = divmod(g, 2)
        o_ref[p, k * hd:(k + 1) * hd, k * hd:(k + 1) * hd] = t[:hd]
        o_ref[p, (2 + k) * hd:(3 + k) * hd, k * hd:(k + 1) * hd] = t[hd:]


def _fold_fmix(w_fmix):
    cs = jnp.asarray(_channel_dft())
    return pl.pallas_call(
        _fold_kernel,
        out_shape=jax.ShapeDtypeStruct((N_GROUPS // 2, 4 * HEAD_DIM, 2 * HEAD_DIM), jnp.bfloat16),
        name="fold_fmix",
    )(cs, w_fmix)


def _vf_kernel(x_ref, scale_ref, shift_ref, w_ref, o_ref):
    h = x_ref[0] * (1.0 + scale_ref[0]) + shift_ref[0]
    o_ref[0] = jnp.dot(h.astype(jnp.bfloat16), w_ref[...],
                       preferred_element_type=jnp.float32).astype(o_ref.dtype)


def _vf_proj(x, mod, row0, w_in, tm):
    b, s, d = x.shape
    return pl.pallas_call(
        _vf_kernel,
        grid=(b, s // tm),
        in_specs=[
            pl.BlockSpec((1, tm, d), lambda i, j: (i, j, 0)),
            pl.BlockSpec((1, 1, d), lambda i, j: (row0 + i, 0, MOD_SCALE)),
            pl.BlockSpec((1, 1, d), lambda i, j: (row0 + i, 0, MOD_SHIFT)),
            pl.BlockSpec((d, D_F), lambda i, j: (0, 5)),
        ],
        out_specs=pl.BlockSpec((1, tm, D_F), lambda i, j: (i, j, 0)),
        out_shape=jax.ShapeDtypeStruct((b, s, D_F), jnp.bfloat16),
        compiler_params=pltpu.CompilerParams(
            dimension_semantics=("arbitrary", "arbitrary"), vmem_limit_bytes=VMEM_LIMIT),
        name="vf_proj",
    )(x, mod, mod, w_in)


def _unroll(trips, mxu_cycles):
    return max(1, min(trips, FFT_BLOCK_CYCLES // mxu_cycles))


def _fft_kernel(v_ref, m1_ref, m2_ref, m3_ref, twr_ref, twi_ref, ur_ref, ui_ref, y2_ref, *, r1):
    k2 = CHUNK * r1
    bf16 = jnp.bfloat16

    def stage1(d1, carry):
        rhs = jnp.concatenate(
            [v_ref[0, pl.ds(pl.multiple_of(k2 * d2 + CHUNK * d1, CHUNK), CHUNK), :] for d2 in range(16)], axis=0)
        r = jnp.dot(m1_ref[...], rhs, preferred_element_type=jnp.float32)
        ur_ref[0, pl.ds(pl.multiple_of(256 * d1, 256), 256), :] = r.astype(bf16)
        return carry

    lax.fori_loop(0, r1, stage1, 0, unroll=_unroll(r1, 64))

    def stage2(d0, carry):
        off = pl.multiple_of(CHUNK * d0, CHUNK)
        rhs = jnp.concatenate([ur_ref[0, pl.ds(256 * d1 + off, CHUNK), :] for d1 in range(r1)], axis=0)
        blk = 2 * E1_GROUP * CHUNK
        for g in range(r1 // E1_GROUP):
            r = jnp.dot(m2_ref[g * blk:(g + 1) * blk, :], rhs, preferred_element_type=jnp.float32)
            for j in range(E1_GROUP):
                e1 = g * E1_GROUP + j
                wr = twr_ref[d0 * r1 + e1]
                wi = twi_ref[d0 * r1 + e1]
                a = r[CHUNK * j:CHUNK * (j + 1)]
                b = r[blk // 2 + CHUNK * j:blk // 2 + CHUNK * (j + 1)]
                y2_ref[d0, 0, CHUNK * e1:CHUNK * (e1 + 1), :] = (a * wr - b * wi).astype(bf16)
                y2_ref[d0, 1, CHUNK * e1:CHUNK * (e1 + 1), :] = (a * wi + b * wr).astype(bf16)
        return carry

    lax.fori_loop(0, 16, stage2, 0, unroll=_unroll(16, r1 * r1 // 2))

    def stage3(e1, carry):
        off = pl.multiple_of(CHUNK * e1, CHUNK)
        rhs = jnp.concatenate(
            [y2_ref[d0, p, pl.ds(off, CHUNK), :] for p in range(2) for d0 in range(16)], axis=0)
        r = jnp.dot(m3_ref[...], rhs, preferred_element_type=jnp.float32).astype(bf16)
        for e2 in range(16):
            row = pl.multiple_of(k2 * e2 + off, CHUNK)
            ur_ref[0, pl.ds(row, CHUNK), :] = r[CHUNK * e2:CHUNK * (e2 + 1)]
            ui_ref[0, pl.ds(row, CHUNK), :] = r[256 + CHUNK * e2:256 + CHUNK * (e2 + 1)]
        return carry

    lax.fori_loop(0, r1, stage3, 0, unroll=_unroll(r1, 256))


def _seq_dft(v):
    b, s, c = v.shape
    r1 = s // 256
    m1, m2, m3, twr, twi = _fft_constants(s)
    m1 = jnp.asarray(m1).astype(jnp.bfloat16)
    m2 = jnp.asarray(m2).astype(jnp.bfloat16)
    m3 = jnp.asarray(m3).astype(jnp.bfloat16)
    smem = pl.BlockSpec(memory_space=pltpu.SMEM)
    blk = pl.BlockSpec((1, s, FFT_LANES), lambda i, j: (i, 0, j))
    const2 = lambda i, j: (0, 0)
    return pl.pallas_call(
        functools.partial(_fft_kernel, r1=r1),
        grid=(b, c // FFT_LANES),
        in_specs=[
            blk,
            pl.BlockSpec(m1.shape, const2),
            pl.BlockSpec(m2.shape, const2),
            pl.BlockSpec(m3.shape, const2),
            smem, smem,
        ],
        out_specs=[blk, blk],
        out_shape=[jax.ShapeDtypeStruct((b, s, c), jnp.bfloat16)] * 2,
        scratch_shapes=[pltpu.VMEM((16, 2, CHUNK * r1, FFT_LANES), jnp.bfloat16)],
        compiler_params=pltpu.CompilerParams(
            dimension_semantics=("arbitrary", "arbitrary"), vmem_limit_bytes=VMEM_LIMIT),
        name="seq_dft",
    )(v, m1, m2, m3, jnp.asarray(twr), jnp.asarray(twi))


def _main_kernel(x_ref, xp_ref, xn_ref, ur_ref, ui_ref, scale_ref, shift_ref, gate_ref,
                 w_ref, g_ref, wo_ref, cw_ref, cb_ref, lg_ref, lb_ref, o_ref, *, tm):
    i = pl.program_id(1)
    n_tiles = pl.num_programs(1)
    bf16 = jnp.bfloat16
    x = x_ref[0]
    s1 = 1.0 + scale_ref[0]
    sh = shift_ref[0]
    h = x * s1 + sh
    h_halo = jnp.concatenate([xp_ref[0], xn_ref[0]], axis=0) * s1 + sh
    h_ext = jnp.concatenate([h, h_halo], axis=0).astype(bf16)

    hb = h_ext[:tm]
    ur = ur_ref[0]
    ui = ui_ref[0]
    rows = lax.broadcasted_iota(jnp.int32, (tm, 1), 0)
    y_a, y_f = [], []
    for c in range(D_A // COL_BLOCK):
        lo, hi = c * COL_BLOCK, (c + 1) * COL_BLOCK
        w0 = c * 5 * COL_BLOCK
        zc = jnp.dot(h_ext, w_ref[:, w0:w0 + 2 * COL_BLOCK], preferred_element_type=jnp.float32)
        u_all = zc[:, :COL_BLOCK] * zc[:, COL_BLOCK:]
        u = u_all[:tm]
        u_prev = jnp.where(i > 0, u_all[tm + 7:tm + 8], 0.0)
        u_next = jnp.where(i < n_tiles - 1, u_all[tm + 8:tm + 9], 0.0)
        u_m1 = jnp.where(rows == 0, u_prev, pltpu.roll(u, 1, 0))
        u_p1 = jnp.where(rows == tm - 1, u_next, pltpu.roll(u, tm - 1, 0))
        conv = (u_m1 * cw_ref[0:1, lo:hi] + u * cw_ref[1:2, lo:hi] + u_p1 * cw_ref[2:3, lo:hi]
                + cb_ref[:, lo:hi])

        zb = jnp.dot(hb, w_ref[:, w0 + 2 * COL_BLOCK:w0 + 5 * COL_BLOCK], preferred_element_type=jnp.float32)
        bg, ga, gf = (zb[:, k * COL_BLOCK:(k + 1) * COL_BLOCK] for k in range(3))
        y_a.append((bg * conv * jax.nn.silu(ga)).astype(bf16))

        lhs = jnp.concatenate([ur[:, lo:hi], ui[:, lo:hi]], axis=1)
        fr = jnp.dot(lhs, g_ref[c], preferred_element_type=jnp.float32)
        y_f.append((fr * jax.nn.silu(gf)).astype(bf16))

    y = jnp.concatenate(y_a + y_f, axis=1)
    for m in range(0, tm, OUT_ROWS):
        o = jnp.dot(y[m:m + OUT_ROWS], wo_ref[...], preferred_element_type=jnp.float32)
        r = ALPHA * x[m:m + OUT_ROWS] + gate_ref[0] * o
        mu = jnp.mean(r, axis=-1, keepdims=True)
        rc = r - mu
        var = jnp.mean(rc * rc, axis=-1, keepdims=True)
        o_ref[0, m:m + OUT_ROWS, :] = rc * lax.rsqrt(var + LN_EPS) * lg_ref[...] + lb_ref[...]


def _main(x, ur, ui, mod, row0, w_in, gfold, wo, conv_w, conv_b, ln_g, ln_b, tm):
    b, s, d = x.shape
    n8 = tm // 8
    last8 = s // 8 - 1
    tile = lambda i, j: (i, j, 0)
    const2 = lambda i, j: (0, 0)
    mod_spec = lambda k: pl.BlockSpec((1, 1, d), lambda i, j: (row0 + i, 0, k))
    return pl.pallas_call(
        functools.partial(_main_kernel, tm=tm),
        grid=(b, s // tm),
        in_specs=[
            pl.BlockSpec((1, tm, d), tile),
            pl.BlockSpec((1, 8, d), lambda i, j: (i, jnp.maximum(j * n8 - 1, 0), 0)),
            pl.BlockSpec((1, 8, d), lambda i, j: (i, jnp.minimum((j + 1) * n8, last8), 0)),
            pl.BlockSpec((1, tm, D_F), tile),
            pl.BlockSpec((1, tm, D_F), tile),
            mod_spec(MOD_SCALE),
            mod_spec(MOD_SHIFT),
            mod_spec(MOD_GATE),
            pl.BlockSpec(w_in.shape, const2),
            pl.BlockSpec(gfold.shape, lambda i, j: (0, 0, 0)),
            pl.BlockSpec(wo.shape, const2),
            pl.BlockSpec(conv_w.shape, const2),
            pl.BlockSpec(conv_b.shape, const2),
            pl.BlockSpec(ln_g.shape, const2),
            pl.BlockSpec(ln_b.shape, const2),
        ],
        out_specs=pl.BlockSpec((1, tm, d), tile),
        out_shape=jax.ShapeDtypeStruct((b, s, d), jnp.float32),
        compiler_params=pltpu.CompilerParams(
            dimension_semantics=("arbitrary", "arbitrary"), vmem_limit_bytes=VMEM_LIMIT),
        name="mixer_main",
    )(x, x, x, ur, ui, mod, mod, mod, w_in, gfold, wo, conv_w, conv_b, ln_g, ln_b)


def kernel(x_prompt, x_sample, c_prompt, c_sample, w_ada, b_ada, w_in, conv_w, conv_b,
           w_fmix, w_out, ln_g, ln_b):
    bf16 = jnp.bfloat16
    w_ada, b_ada, w_in, conv_w, conv_b = w_ada[0], b_ada[0], w_in[0], conv_w[0], conv_b[0]
    w_fmix, w_out, ln_g, ln_b = w_fmix[0], w_out[0], ln_g[0], ln_b[0]

    w6 = w_in.astype(bf16).reshape(D_MODEL, 6, D_A // COL_BLOCK, COL_BLOCK)
    blocks = jnp.stack([w6[:, k] for k in (COL_CG, COL_VA, COL_BG, COL_GA, COL_GF)], axis=2)
    w_in = jnp.concatenate([blocks.reshape(D_MODEL, 5 * D_A), w6[:, COL_VF].reshape(D_MODEL, D_F)], axis=1)
    wo = w_out.astype(bf16)
    gfold = _fold_fmix(w_fmix)

    nb = c_prompt.shape[0]
    mod = _modulation(jnp.concatenate([c_prompt, c_sample], axis=0), w_ada, b_ada)
    mod = mod.reshape(mod.shape[0], 1, 3 * D_MODEL)

    outs = []
    for x, row0 in ((x_prompt, 0), (x_sample, nb)):
        vf = _vf_proj(x, mod, row0, w_in, tm=VF_ROWS)
        ur, ui = _seq_dft(vf)
        outs.append(_main(x, ur, ui, mod, row0, w_in, gfold, wo, conv_w, conv_b.reshape(1, -1),
                          ln_g.reshape(1, -1), ln_b.reshape(1, -1), tm=MAIN_ROWS))
    return tuple(outs)
```

```python
import functools

import numpy as np
import jax
import jax.numpy as jnp
from jax import lax
from jax.experimental import pallas as pl
from jax.experimental.pallas import tpu as pltpu

D_MODEL = 1024
HEAD_DIM = 128
N_GROUPS = 8
D_A = 1024
D_F = 1024
ALPHA = float(2.0 ** 0.25)
LN_EPS = 1e-5

COL_BG, COL_CG, COL_VA, COL_GA, COL_VF, COL_GF = range(6)
MOD_SHIFT, MOD_SCALE, MOD_GATE = range(3)
VF_ROWS = 2048
MAIN_ROWS = 1024
OUT_ROWS = 256
COL_BLOCK = 256

CHUNK = 16
FFT_LANES = 256
VMEM_LIMIT = 56 * 1024 * 1024
FFT_BLOCK_CYCLES = 2048
E1_GROUP = 8


def _fft_constants(seq):
    r1 = seq // 256
    eye = np.eye(16)
    d = np.arange(16)
    pk = np.where(d[:, None] <= 8, np.cos(2 * np.pi * d[None, :] * d[:, None] / 16),
                  -np.sin(2 * np.pi * d[None, :] * (d[:, None] - 8) / 16))
    m1 = np.einsum('ab,pd->apdb', eye, pk).reshape(256, 256)

    unpack = np.zeros((16, 16), complex)
    unpack[0, 0] = unpack[8, 8] = 1
    for e0 in range(1, 8):
        unpack[e0, e0], unpack[e0, 8 + e0] = 1, 1j
        unpack[16 - e0, e0], unpack[16 - e0, 8 + e0] = 1, -1j
    dd = np.arange(r1)
    ph = dd[None, None, :] * dd[:, None, None] / r1 + dd[None, None, :] * d[None, :, None] / (16.0 * r1)
    c = np.exp(-2j * np.pi * ph)
    m2c = np.einsum('aed,ep->aedp', c, unpack).reshape(16 * r1, 16 * r1)
    blk = E1_GROUP * 16
    m2 = np.concatenate([np.concatenate([m2c.real[i:i + blk], m2c.imag[i:i + blk]], axis=0)
                         for i in range(0, 16 * r1, blk)], axis=0)

    ph = (d[None, None, :] * d[:, None, None]) / 16.0 + d[None, None, :] * d[None, :, None] / float(seq)
    c = np.exp(-2j * np.pi * ph) / np.sqrt(seq)
    cf = np.einsum('ald,lm->aldm', c, eye).reshape(256, 256)
    m3 = np.stack([cf.real, cf.imag, cf.real + cf.imag])

    tw = np.exp(-2j * np.pi * np.outer(d, dd) / (16.0 * r1))
    f32 = np.float32
    return (m1.astype(f32), m2.astype(f32), m3.astype(f32),
            tw.real.reshape(-1).astype(f32), tw.imag.reshape(-1).astype(f32))


def _channel_dft():
    n = np.arange(HEAD_DIM)
    ang = 2 * np.pi * np.outer(n, n) / HEAD_DIM
    cs = np.concatenate([np.cos(ang), np.sin(ang)], axis=0) / np.sqrt(HEAD_DIM)
    return cs.astype(np.float32)


def _mod_kernel(c_ref, w_ref, b_ref, o_ref):
    c = c_ref[...]
    o_ref[...] = jnp.dot(jax.nn.silu(c), w_ref[...], precision=lax.Precision.HIGHEST,
                         preferred_element_type=jnp.float32) + b_ref[...]


def _modulation(c_all, w_ada, b_ada):
    n, d = c_all.shape
    tn = 512
    return pl.pallas_call(
        _mod_kernel,
        grid=(3 * D_MODEL // tn,),
        in_specs=[
            pl.BlockSpec((n, d), lambda j: (0, 0)),
            pl.BlockSpec((d, tn), lambda j: (0, j)),
            pl.BlockSpec((1, tn), lambda j: (0, j)),
        ],
        out_specs=pl.BlockSpec((n, tn), lambda j: (0, j)),
        out_shape=jax.ShapeDtypeStruct((n, 3 * D_MODEL), jnp.float32),
        compiler_params=pltpu.CompilerParams(dimension_semantics=("arbitrary",)),
        name="adaln_mod",
    )(c_all, w_ada, b_ada.reshape(1, -1))


def _fold_kernel(cs_ref, w_ref, o_ref):
    hd = HEAD_DIM
    o_ref[...] = jnp.zeros(o_ref.shape, o_ref.dtype)
    for g in range(N_GROUPS):
        t = jnp.dot(cs_ref[...], w_ref[g], precision=lax.Precision.HIGHEST,
                    preferred_element_type=jnp.float32).astype(o_ref.dtype)
        p, k = divmod(g, 2)
        o_ref[p, k * hd:(k + 1) * hd, k * hd:(k + 1) * hd] = t[:hd]
        o_ref[p, (2 + k) * hd:(3 + k) * hd, k * hd:(k + 1) * hd] = t[hd:]


def _fold_fmix(w_fmix):
    cs = jnp.asarray(_channel_dft())
    return pl.pallas_call(
        _fold_kernel,
        out_shape=jax.ShapeDtypeStruct((N_GROUPS // 2, 4 * HEAD_DIM, 2 * HEAD_DIM), jnp.bfloat16),
        name="fold_fmix",
    )(cs, w_fmix)


def _regroup_kernel(cg_ref, va_ref, bg_ref, ga_ref, gf_ref, vf_ref, main_ref, vfo_ref):
    for k, ref in enumerate((cg_ref, va_ref, bg_ref, ga_ref, gf_ref)):
        main_ref[:, k * COL_BLOCK:(k + 1) * COL_BLOCK] = ref[...].astype(main_ref.dtype)
    vfo_ref[...] = vf_ref[...].astype(vfo_ref.dtype)


def _regroup_w_in(w_in):
    d = w_in.shape[0]
    n_blocks = D_A // COL_BLOCK
    src = lambda k: pl.BlockSpec((d, COL_BLOCK), lambda c: (0, k * n_blocks + c))
    return pl.pallas_call(
        _regroup_kernel,
        grid=(n_blocks,),
        in_specs=[src(k) for k in (COL_CG, COL_VA, COL_BG, COL_GA, COL_GF, COL_VF)],
        out_specs=[pl.BlockSpec((d, 5 * COL_BLOCK), lambda c: (0, c)),
                   pl.BlockSpec((d, COL_BLOCK), lambda c: (0, c))],
        out_shape=[jax.ShapeDtypeStruct((d, 5 * D_A), jnp.bfloat16),
                   jax.ShapeDtypeStruct((d, D_F), jnp.bfloat16)],
        compiler_params=pltpu.CompilerParams(dimension_semantics=("arbitrary",)),
        name="regroup_w_in",
    )(w_in, w_in, w_in, w_in, w_in, w_in)


def _vf_kernel(x_ref, scale_ref, shift_ref, w_ref, o_ref):
    h = x_ref[0] * (1.0 + scale_ref[0]) + shift_ref[0]
    o_ref[0] = jnp.dot(h.astype(jnp.bfloat16), w_ref[...],
                       preferred_element_type=jnp.float32).astype(o_ref.dtype)


def _vf_proj(x, mod, row0, w_in, tm):
    b, s, d = x.shape
    return pl.pallas_call(
        _vf_kernel,
        grid=(b, s // tm),
        in_specs=[
            pl.BlockSpec((1, tm, d), lambda i, j: (i, j, 0)),
            pl.BlockSpec((1, 1, d), lambda i, j: (row0 + i, 0, MOD_SCALE)),
            pl.BlockSpec((1, 1, d), lambda i, j: (row0 + i, 0, MOD_SHIFT)),
            pl.BlockSpec((d, D_F), lambda i, j: (0, 0)),
        ],
        out_specs=pl.BlockSpec((1, tm, D_F), lambda i, j: (i, j, 0)),
        out_shape=jax.ShapeDtypeStruct((b, s, D_F), jnp.bfloat16),
        compiler_params=pltpu.CompilerParams(
            dimension_semantics=("arbitrary", "arbitrary"), vmem_limit_bytes=VMEM_LIMIT),
        name="vf_proj",
    )(x, mod, mod, w_in)


def _unroll(trips, mxu_cycles):
    want = max(1, min(trips, FFT_BLOCK_CYCLES // mxu_cycles))
    return max(u for u in range(1, want + 1) if trips % u == 0)


def _fft_kernel(v_ref, m1_ref, m2_ref, m3_ref, twr_ref, twi_ref, ur_ref, ui_ref, y2_ref, *, r1):
    k2 = CHUNK * r1
    bf16 = jnp.bfloat16

    def stage1(d1, carry):
        rhs = jnp.concatenate(
            [v_ref[0, pl.ds(pl.multiple_of(k2 * d2 + CHUNK * d1, CHUNK), CHUNK), :] for d2 in range(16)], axis=0)
        r = jnp.dot(m1_ref[...], rhs, preferred_element_type=jnp.float32)
        ur_ref[0, pl.ds(pl.multiple_of(256 * d1, 256), 256), :] = r.astype(bf16)
        return carry

    lax.fori_loop(0, r1, stage1, 0, unroll=_unroll(r1, 64))

    def stage2(d0, carry):
        off = pl.multiple_of(CHUNK * d0, CHUNK)
        rhs = jnp.concatenate([ur_ref[0, pl.ds(256 * d1 + off, CHUNK), :] for d1 in range(r1)], axis=0)
        blk = 2 * E1_GROUP * CHUNK
        for g in range(r1 // E1_GROUP):
            r = jnp.dot(m2_ref[g * blk:(g + 1) * blk, :], rhs, preferred_element_type=jnp.float32)
            for j in range(E1_GROUP):
                e1 = g * E1_GROUP + j
                wr = twr_ref[d0 * r1 + e1]
                wi = twi_ref[d0 * r1 + e1]
                a = r[CHUNK * j:CHUNK * (j + 1)]
                b = r[blk // 2 + CHUNK * j:blk // 2 + CHUNK * (j + 1)]
                yr = a * wr - b * wi
                yi = a * wi + b * wr
                for p, val in enumerate((yr, yi, yr + yi)):
                    y2_ref[d0, p, CHUNK * e1:CHUNK * (e1 + 1), :] = val.astype(bf16)
        return carry

    lax.fori_loop(0, 16, stage2, 0, unroll=_unroll(16, r1 * r1 // 2))

    def stage3(e1, carry):
        off = pl.multiple_of(CHUNK * e1, CHUNK)
        t1, t2, t3 = (
            jnp.dot(m3_ref[p], jnp.concatenate([y2_ref[d0, p, pl.ds(off, CHUNK), :] for d0 in range(16)], axis=0),
                    preferred_element_type=jnp.float32)
            for p in range(3))
        re = (t1 - t2).astype(bf16)
        im = (t3 - t1 - t2).astype(bf16)
        for e2 in range(16):
            row = pl.multiple_of(k2 * e2 + off, CHUNK)
            ur_ref[0, pl.ds(row, CHUNK), :] = re[CHUNK * e2:CHUNK * (e2 + 1)]
            ui_ref[0, pl.ds(row, CHUNK), :] = im[CHUNK * e2:CHUNK * (e2 + 1)]
        return carry

    lax.fori_loop(0, r1, stage3, 0, unroll=_unroll(r1, 192))


def _seq_dft(v):
    b, s, c = v.shape
    r1 = s // 256
    m1, m2, m3, twr, twi = _fft_constants(s)
    m1 = jnp.asarray(m1).astype(jnp.bfloat16)
    m2 = jnp.asarray(m2).astype(jnp.bfloat16)
    m3 = jnp.asarray(m3).astype(jnp.bfloat16)
    smem = pl.BlockSpec(memory_space=pltpu.SMEM)
    blk = pl.BlockSpec((1, s, FFT_LANES), lambda i, j: (i, 0, j))
    const2 = lambda i, j: (0, 0)
    return pl.pallas_call(
        functools.partial(_fft_kernel, r1=r1),
        grid=(b, c // FFT_LANES),
        in_specs=[
            blk,
            pl.BlockSpec(m1.shape, const2),
            pl.BlockSpec(m2.shape, const2),
            pl.BlockSpec(m3.shape, lambda i, j: (0, 0, 0)),
            smem, smem,
        ],
        out_specs=[blk, blk],
        out_shape=[jax.ShapeDtypeStruct((b, s, c), jnp.bfloat16)] * 2,
        scratch_shapes=[pltpu.VMEM((16, 3, CHUNK * r1, FFT_LANES), jnp.bfloat16)],
        compiler_params=pltpu.CompilerParams(
            dimension_semantics=("arbitrary", "arbitrary"), vmem_limit_bytes=VMEM_LIMIT),
        name="seq_dft",
    )(v, m1, m2, m3, jnp.asarray(twr), jnp.asarray(twi))


def _main_kernel(x_ref, xp_ref, xn_ref, ur_ref, ui_ref, scale_ref, shift_ref, gate_ref,
                 w_ref, g_ref, wo_ref, cw_ref, cb_ref, lg_ref, lb_ref, o_ref, *, tm):
    i = pl.program_id(1)
    n_tiles = pl.num_programs(1)
    bf16 = jnp.bfloat16
    x = x_ref[0]
    s1 = 1.0 + scale_ref[0]
    sh = shift_ref[0]
    h = x * s1 + sh
    h_halo = jnp.concatenate([xp_ref[0], xn_ref[0]], axis=0) * s1 + sh
    h_ext = jnp.concatenate([h, h_halo], axis=0).astype(bf16)

    hb = h_ext[:tm]
    ur = ur_ref[0]
    ui = ui_ref[0]
    rows = lax.broadcasted_iota(jnp.int32, (tm, 1), 0)
    y_a, y_f = [], []
    for c in range(D_A // COL_BLOCK):
        lo, hi = c * COL_BLOCK, (c + 1) * COL_BLOCK
        w0 = c * 5 * COL_BLOCK
        zc = jnp.dot(h_ext, w_ref[:, w0:w0 + 2 * COL_BLOCK], preferred_element_type=jnp.float32)
        u_all = zc[:, :COL_BLOCK] * zc[:, COL_BLOCK:]
        u = u_all[:tm]
        u_prev = jnp.where(i > 0, u_all[tm + 7:tm + 8], 0.0)
        u_next = jnp.where(i < n_tiles - 1, u_all[tm + 8:tm + 9], 0.0)
        u_m1 = jnp.where(rows == 0, u_prev, pltpu.roll(u, 1, 0))
        u_p1 = jnp.where(rows == tm - 1, u_next, pltpu.roll(u, tm - 1, 0))
        conv = (u_m1 * cw_ref[0:1, lo:hi] + u * cw_ref[1:2, lo:hi] + u_p1 * cw_ref[2:3, lo:hi]
                + cb_ref[:, lo:hi])

        zb = jnp.dot(hb, w_ref[:, w0 + 2 * COL_BLOCK:w0 + 5 * COL_BLOCK], preferred_element_type=jnp.float32)
        bg, ga, gf = (zb[:, k * COL_BLOCK:(k + 1) * COL_BLOCK] for k in range(3))
        y_a.append((bg * conv * jax.nn.silu(ga)).astype(bf16))

        lhs = jnp.concatenate([ur[:, lo:hi], ui[:, lo:hi]], axis=1)
        fr = jnp.dot(lhs, g_ref[c], preferred_element_type=jnp.float32)
        y_f.append((fr * jax.nn.silu(gf)).astype(bf16))

    y = jnp.concatenate(y_a + y_f, axis=1)
    for m in range(0, tm, OUT_ROWS):
        o = jnp.dot(y[m:m + OUT_ROWS], wo_ref[...], preferred_element_type=jnp.float32)
        r = ALPHA * x[m:m + OUT_ROWS] + gate_ref[0] * o
        mu = jnp.mean(r, axis=-1, keepdims=True)
        rc = r - mu
        var = jnp.mean(rc * rc, axis=-1, keepdims=True)
        o_ref[0, m:m + OUT_ROWS, :] = rc * lax.rsqrt(var + LN_EPS) * lg_ref[...] + lb_ref[...]


def _main(x, ur, ui, mod, row0, w_in, gfold, wo, conv_w, conv_b, ln_g, ln_b, tm):
    b, s, d = x.shape
    n8 = tm // 8
    last8 = s // 8 - 1
    tile = lambda i, j: (i, j, 0)
    const2 = lambda i, j: (0, 0)
    mod_spec = lambda k: pl.BlockSpec((1, 1, d), lambda i, j: (row0 + i, 0, k))
    return pl.pallas_call(
        functools.partial(_main_kernel, tm=tm),
        grid=(b, s // tm),
        in_specs=[
            pl.BlockSpec((1, tm, d), tile),
            pl.BlockSpec((1, 8, d), lambda i, j: (i, jnp.maximum(j * n8 - 1, 0), 0)),
            pl.BlockSpec((1, 8, d), lambda i, j: (i, jnp.minimum((j + 1) * n8, last8), 0)),
            pl.BlockSpec((1, tm, D_F), tile),
            pl.BlockSpec((1, tm, D_F), tile),
            mod_spec(MOD_SCALE),
            mod_spec(MOD_SHIFT),
            mod_spec(MOD_GATE),
            pl.BlockSpec(w_in.shape, const2),
            pl.BlockSpec(gfold.shape, lambda i, j: (0, 0, 0)),
            pl.BlockSpec(wo.shape, const2),
            pl.BlockSpec(conv_w.shape, const2),
            pl.BlockSpec(conv_b.shape, const2),
            pl.BlockSpec(ln_g.shape, const2),
            pl.BlockSpec(ln_b.shape, const2),
        ],
        out_specs=pl.BlockSpec((1, tm, d), tile),
        out_shape=jax.ShapeDtypeStruct((b, s, d), jnp.float32),
        compiler_params=pltpu.CompilerParams(
            dimension_semantics=("arbitrary", "arbitrary"), vmem_limit_bytes=VMEM_LIMIT),
        name="mixer_main",
    )(x, x, x, ur, ui, mod, mod, mod, w_in, gfold, wo, conv_w, conv_b, ln_g, ln_b)


def kernel(x_prompt, x_sample, c_prompt, c_sample, w_ada, b_ada, w_in, conv_w, conv_b,
           w_fmix, w_out, ln_g, ln_b):
    bf16 = jnp.bfloat16
    w_ada, b_ada, w_in, conv_w, conv_b = w_ada[0], b_ada[0], w_in[0], conv_w[0], conv_b[0]
    w_fmix, w_out, ln_g, ln_b = w_fmix[0], w_out[0], ln_g[0], ln_b[0]

    w_main, w_vf = _regroup_w_in(w_in)
    wo = w_out.astype(bf16)
    gfold = _fold_fmix(w_fmix)

    nb = c_prompt.shape[0]
    mod = _modulation(jnp.concatenate([c_prompt, c_sample], axis=0), w_ada, b_ada)
    mod = mod.reshape(mod.shape[0], 1, 3 * D_MODEL)

    outs = []
    for x, row0 in ((x_prompt, 0), (x_sample, nb)):
        vf = _vf_proj(x, mod, row0, w_vf, tm=VF_ROWS)
        ur, ui = _seq_dft(vf)
        outs.append(_main(x, ur, ui, mod, row0, w_main, gfold, wo, conv_w, conv_b.reshape(1, -1),
                          ln_g.reshape(1, -1), ln_b.reshape(1, -1), tm=MAIN_ROWS))
    return tuple(outs)
```

```python
import functools

import numpy as np
import jax
import jax.numpy as jnp
from jax import lax
from jax.experimental import pallas as pl
from jax.experimental.pallas import tpu as pltpu

D_MODEL = 1024
HEAD_DIM = 128
N_GROUPS = 8
D_A = 1024
D_F = 1024
ALPHA = float(2.0 ** 0.25)
LN_EPS = 1e-5

COL_BG, COL_CG, COL_VA, COL_GA, COL_VF, COL_GF = range(6)
MOD_SHIFT, MOD_SCALE, MOD_GATE = range(3)
VF_ROWS = 2048
MAIN_ROWS = 1024
OUT_ROWS = 256
COL_BLOCK = 256

CHUNK = 16
FFT_LANES = 256
VMEM_LIMIT = 56 * 1024 * 1024
FFT_BLOCK_CYCLES = 2048
E1_GROUP = 8


def _fft_constants(seq):
    r1 = seq // 256
    eye = np.eye(16)
    d = np.arange(16)
    pk = np.where(d[:, None] <= 8, np.cos(2 * np.pi * d[None, :] * d[:, None] / 16),
                  -np.sin(2 * np.pi * d[None, :] * (d[:, None] - 8) / 16))
    m1 = np.einsum('ab,pd->apdb', eye, pk).reshape(256, 256)

    unpack = np.zeros((16, 16), complex)
    unpack[0, 0] = unpack[8, 8] = 1
    for e0 in range(1, 8):
        unpack[e0, e0], unpack[e0, 8 + e0] = 1, 1j
        unpack[16 - e0, e0], unpack[16 - e0, 8 + e0] = 1, -1j
    dd = np.arange(r1)
    ph = dd[None, None, :] * dd[:, None, None] / r1 + dd[None, None, :] * d[None, :, None] / (16.0 * r1)
    c = np.exp(-2j * np.pi * ph)
    m2c = np.einsum('aed,ep->aedp', c, unpack).reshape(16 * r1, 16 * r1)
    blk = E1_GROUP * 16
    m2 = np.concatenate([np.concatenate([m2c.real[i:i + blk], m2c.imag[i:i + blk]], axis=0)
                         for i in range(0, 16 * r1, blk)], axis=0)

    ph = (d[None, None, :] * d[:, None, None]) / 16.0 + d[None, None, :] * d[None, :, None] / float(seq)
    c = np.exp(-2j * np.pi * ph) / np.sqrt(seq)
    cf = np.einsum('ald,lm->aldm', c, eye).reshape(256, 256)
    m3 = np.stack([cf.real, cf.imag])

    tw = np.repeat(np.exp(-2j * np.pi * np.outer(dd, d) / (16.0 * r1)), 16, axis=1)
    f32 = np.float32
    return m1.astype(f32), m2.astype(f32), m3.astype(f32), tw.real.astype(f32), tw.imag.astype(f32)


def _channel_dft():
    n = np.arange(HEAD_DIM)
    ang = 2 * np.pi * np.outer(n, n) / HEAD_DIM
    cs = np.concatenate([np.cos(ang), np.sin(ang)], axis=0) / np.sqrt(HEAD_DIM)
    return cs.astype(np.float32)


def _mod_kernel(c_ref, w_ref, b_ref, o_ref):
    c = c_ref[...]
    o_ref[...] = jnp.dot(jax.nn.silu(c), w_ref[...], precision=lax.Precision.HIGHEST,
                         preferred_element_type=jnp.float32) + b_ref[...]


def _modulation(c_all, w_ada, b_ada):
    n, d = c_all.shape
    tn = 512
    return pl.pallas_call(
        _mod_kernel,
        grid=(3 * D_MODEL // tn,),
        in_specs=[
            pl.BlockSpec((n, d), lambda j: (0, 0)),
            pl.BlockSpec((d, tn), lambda j: (0, j)),
            pl.BlockSpec((1, tn), lambda j: (0, j)),
        ],
        out_specs=pl.BlockSpec((n, tn), lambda j: (0, j)),
        out_shape=jax.ShapeDtypeStruct((n, 3 * D_MODEL), jnp.float32),
        compiler_params=pltpu.CompilerParams(dimension_semantics=("arbitrary",)),
        name="adaln_mod",
    )(c_all, w_ada, b_ada.reshape(1, -1))


def _fold_kernel(cs_ref, w_ref, o_ref):
    hd = HEAD_DIM
    o_ref[...] = jnp.zeros(o_ref.shape, o_ref.dtype)
    for g in range(N_GROUPS):
        t = jnp.dot(cs_ref[...], w_ref[g], precision=lax.Precision.HIGHEST,
                    preferred_element_type=jnp.float32).astype(o_ref.dtype)
        p, k = divmod(g, 2)
        o_ref[p, k * hd:(k + 1) * hd, k * hd:(k + 1) * hd] = t[:hd]
        o_ref[p, (2 + k) * hd:(3 + k) * hd, k * hd:(k + 1) * hd] = t[hd:]


def _fold_fmix(w_fmix):
    cs = jnp.asarray(_channel_dft())
    return pl.pallas_call(
        _fold_kernel,
        out_shape=jax.ShapeDtypeStruct((N_GROUPS // 2, 4 * HEAD_DIM, 2 * HEAD_DIM), jnp.bfloat16),
        name="fold_fmix",
    )(cs, w_fmix)


def _regroup_kernel(cg_ref, va_ref, bg_ref, ga_ref, gf_ref, vf_ref, main_ref, vfo_ref):
    for k, ref in enumerate((cg_ref, va_ref, bg_ref, ga_ref, gf_ref)):
        main_ref[:, k * COL_BLOCK:(k + 1) * COL_BLOCK] = ref[...].astype(main_ref.dtype)
    vfo_ref[...] = vf_ref[...].astype(vfo_ref.dtype)


def _regroup_w_in(w_in):
    d = w_in.shape[0]
    n_blocks = D_A // COL_BLOCK
    src = lambda k: pl.BlockSpec((d, COL_BLOCK), lambda c: (0, k * n_blocks + c))
    return pl.pallas_call(
        _regroup_kernel,
        grid=(n_blocks,),
        in_specs=[src(k) for k in (COL_CG, COL_VA, COL_BG, COL_GA, COL_GF, COL_VF)],
        out_specs=[pl.BlockSpec((d, 5 * COL_BLOCK), lambda c: (0, c)),
                   pl.BlockSpec((d, COL_BLOCK), lambda c: (0, c))],
        out_shape=[jax.ShapeDtypeStruct((d, 5 * D_A), jnp.bfloat16),
                   jax.ShapeDtypeStruct((d, D_F), jnp.bfloat16)],
        compiler_params=pltpu.CompilerParams(dimension_semantics=("arbitrary",)),
        name="regroup_w_in",
    )(w_in, w_in, w_in, w_in, w_in, w_in)


def _vf_kernel(x_ref, scale_ref, shift_ref, w_ref, o_ref):
    h = x_ref[0] * (1.0 + scale_ref[0]) + shift_ref[0]
    o_ref[0] = jnp.dot(h.astype(jnp.bfloat16), w_ref[...],
                       preferred_element_type=jnp.float32).astype(o_ref.dtype)


def _vf_proj(x, mod, row0, w_in, tm):
    b, s, d = x.shape
    return pl.pallas_call(
        _vf_kernel,
        grid=(b, s // tm),
        in_specs=[
            pl.BlockSpec((1, tm, d), lambda i, j: (i, j, 0)),
            pl.BlockSpec((1, 1, d), lambda i, j: (row0 + i, 0, MOD_SCALE)),
            pl.BlockSpec((1, 1, d), lambda i, j: (row0 + i, 0, MOD_SHIFT)),
            pl.BlockSpec((d, D_F), lambda i, j: (0, 0)),
        ],
        out_specs=pl.BlockSpec((1, tm, D_F), lambda i, j: (i, j, 0)),
        out_shape=jax.ShapeDtypeStruct((b, s, D_F), jnp.bfloat16),
        compiler_params=pltpu.CompilerParams(
            dimension_semantics=("arbitrary", "arbitrary"), vmem_limit_bytes=VMEM_LIMIT),
        name="vf_proj",
    )(x, mod, mod, w_in)


def _unroll(trips, mxu_cycles):
    want = max(1, min(trips, FFT_BLOCK_CYCLES // mxu_cycles))
    return max(u for u in range(1, want + 1) if trips % u == 0)


def _fft_kernel(v_ref, m1_ref, m2_ref, m3_ref, twr_ref, twi_ref, ur_ref, ui_ref, y2_ref, m3v_ref, *, r1):
    k2 = CHUNK * r1
    bf16 = jnp.bfloat16

    @pl.when((pl.program_id(0) == 0) & (pl.program_id(1) == 0))
    def _():
        def fold(e1, carry):
            wr = twr_ref[pl.ds(e1, 1), :]
            wi = twi_ref[pl.ds(e1, 1), :]
            cr = m3_ref[0] * wr - m3_ref[1] * wi
            ci = m3_ref[0] * wi + m3_ref[1] * wr
            for p, val in enumerate((cr, ci, cr + ci)):
                m3v_ref[e1, p] = val.astype(bf16)
            return carry
        lax.fori_loop(0, r1, fold, 0)

    def stage1(d1, carry):
        rhs = jnp.concatenate(
            [v_ref[0, pl.ds(pl.multiple_of(k2 * d2 + CHUNK * d1, CHUNK), CHUNK), :] for d2 in range(16)], axis=0)
        r = jnp.dot(m1_ref[...], rhs, preferred_element_type=jnp.float32)
        ur_ref[0, pl.ds(pl.multiple_of(256 * d1, 256), 256), :] = r.astype(bf16)
        return carry

    lax.fori_loop(0, r1, stage1, 0, unroll=_unroll(r1, 64))

    def stage2(d0, carry):
        off = pl.multiple_of(CHUNK * d0, CHUNK)
        rhs = jnp.concatenate([ur_ref[0, pl.ds(256 * d1 + off, CHUNK), :] for d1 in range(r1)], axis=0)
        blk = 2 * E1_GROUP * CHUNK
        half = blk // 2
        for g in range(r1 // E1_GROUP):
            r = jnp.dot(m2_ref[g * blk:(g + 1) * blk, :], rhs, preferred_element_type=jnp.float32)
            yr, yi = r[:half], r[half:]
            for p, val in enumerate((yr, yi, yr + yi)):
                y2_ref[d0, p, g * half:(g + 1) * half, :] = val.astype(bf16)
        return carry

    lax.fori_loop(0, 16, stage2, 0, unroll=_unroll(16, r1 * r1 // 2))

    def stage3(e1, carry):
        off = pl.multiple_of(CHUNK * e1, CHUNK)
        t1, t2, t3 = (
            jnp.dot(m3v_ref[e1, p],
                    jnp.concatenate([y2_ref[d0, p, pl.ds(off, CHUNK), :] for d0 in range(16)], axis=0),
                    preferred_element_type=jnp.float32)
            for p in range(3))
        re = (t1 - t2).astype(bf16)
        im = (t3 - t1 - t2).astype(bf16)
        for e2 in range(16):
            row = pl.multiple_of(k2 * e2 + off, CHUNK)
            ur_ref[0, pl.ds(row, CHUNK), :] = re[CHUNK * e2:CHUNK * (e2 + 1)]
            ui_ref[0, pl.ds(row, CHUNK), :] = im[CHUNK * e2:CHUNK * (e2 + 1)]
        return carry

    lax.fori_loop(0, r1, stage3, 0, unroll=_unroll(r1, 192))


def _seq_dft(v):
    b, s, c = v.shape
    r1 = s // 256
    m1, m2, m3, twr, twi = _fft_constants(s)
    m1 = jnp.asarray(m1).astype(jnp.bfloat16)
    m2 = jnp.asarray(m2).astype(jnp.bfloat16)
    blk = pl.BlockSpec((1, s, FFT_LANES), lambda i, j: (i, 0, j))
    const2 = lambda i, j: (0, 0)
    return pl.pallas_call(
        functools.partial(_fft_kernel, r1=r1),
        grid=(b, c // FFT_LANES),
        in_specs=[
            blk,
            pl.BlockSpec(m1.shape, const2),
            pl.BlockSpec(m2.shape, const2),
            pl.BlockSpec(m3.shape, lambda i, j: (0, 0, 0)),
            pl.BlockSpec(twr.shape, const2),
            pl.BlockSpec(twi.shape, const2),
        ],
        out_specs=[blk, blk],
        out_shape=[jax.ShapeDtypeStruct((b, s, c), jnp.bfloat16)] * 2,
        scratch_shapes=[pltpu.VMEM((16, 3, CHUNK * r1, FFT_LANES), jnp.bfloat16),
                        pltpu.VMEM((r1, 3, 256, 256), jnp.bfloat16)],
        compiler_params=pltpu.CompilerParams(
            dimension_semantics=("arbitrary", "arbitrary"), vmem_limit_bytes=VMEM_LIMIT),
        name="seq_dft",
    )(v, m1, m2, jnp.asarray(m3), jnp.asarray(twr), jnp.asarray(twi))


def _main_kernel(x_ref, xp_ref, xn_ref, ur_ref, ui_ref, scale_ref, shift_ref, gate_ref,
                 w_ref, g_ref, wo_ref, cw_ref, cb_ref, lg_ref, lb_ref, o_ref, *, tm):
    i = pl.program_id(1)
    n_tiles = pl.num_programs(1)
    bf16 = jnp.bfloat16
    x = x_ref[0]
    s1 = 1.0 + scale_ref[0]
    sh = shift_ref[0]
    h = x * s1 + sh
    h_halo = jnp.concatenate([xp_ref[0], xn_ref[0]], axis=0) * s1 + sh
    h_ext = jnp.concatenate([h, h_halo], axis=0).astype(bf16)

    hb = h_ext[:tm]
    ur = ur_ref[0]
    ui = ui_ref[0]
    rows = lax.broadcasted_iota(jnp.int32, (tm, 1), 0)
    y_a, y_f = [], []
    for c in range(D_A // COL_BLOCK):
        lo, hi = c * COL_BLOCK, (c + 1) * COL_BLOCK
        w0 = c * 5 * COL_BLOCK
        zc = jnp.dot(h_ext, w_ref[:, w0:w0 + 2 * COL_BLOCK], preferred_element_type=jnp.float32)
        u_all = zc[:, :COL_BLOCK] * zc[:, COL_BLOCK:]
        u = u_all[:tm]
        u_prev = jnp.where(i > 0, u_all[tm + 7:tm + 8], 0.0)
        u_next = jnp.where(i < n_tiles - 1, u_all[tm + 8:tm + 9], 0.0)
        u_m1 = jnp.where(rows == 0, u_prev, pltpu.roll(u, 1, 0))
        u_p1 = jnp.where(rows == tm - 1, u_next, pltpu.roll(u, tm - 1, 0))
        conv = (u_m1 * cw_ref[0:1, lo:hi] + u * cw_ref[1:2, lo:hi] + u_p1 * cw_ref[2:3, lo:hi]
                + cb_ref[:, lo:hi])

        zb = jnp.dot(hb, w_ref[:, w0 + 2 * COL_BLOCK:w0 + 5 * COL_BLOCK], preferred_element_type=jnp.float32)
        bg, ga, gf = (zb[:, k * COL_BLOCK:(k + 1) * COL_BLOCK] for k in range(3))
        y_a.append((bg * conv * jax.nn.silu(ga)).astype(bf16))

        lhs = jnp.concatenate([ur[:, lo:hi], ui[:, lo:hi]], axis=1)
        fr = jnp.dot(lhs, g_ref[c], preferred_element_type=jnp.float32)
        y_f.append((fr * jax.nn.silu(gf)).astype(bf16))

    y = jnp.concatenate(y_a + y_f, axis=1)
    for m in range(0, tm, OUT_ROWS):
        o = jnp.dot(y[m:m + OUT_ROWS], wo_ref[...], preferred_element_type=jnp.float32)
        r = ALPHA * x[m:m + OUT_ROWS] + gate_ref[0] * o
        mu = jnp.mean(r, axis=-1, keepdims=True)
        rc = r - mu
        var = jnp.mean(rc * rc, axis=-1, keepdims=True)
        o_ref[0, m:m + OUT_ROWS, :] = rc * lax.rsqrt(var + LN_EPS) * lg_ref[...] + lb_ref[...]


def _main(x, ur, ui, mod, row0, w_in, gfold, wo, conv_w, conv_b, ln_g, ln_b, tm):
    b, s, d = x.shape
    n8 = tm // 8
    last8 = s // 8 - 1
    tile = lambda i, j: (i, j, 0)
    const2 = lambda i, j: (0, 0)
    mod_spec = lambda k: pl.BlockSpec((1, 1, d), lambda i, j: (row0 + i, 0, k))
    return pl.pallas_call(
        functools.partial(_main_kernel, tm=tm),
        grid=(b, s // tm),
        in_specs=[
            pl.BlockSpec((1, tm, d), tile),
            pl.BlockSpec((1, 8, d), lambda i, j: (i, jnp.maximum(j * n8 - 1, 0), 0)),
            pl.BlockSpec((1, 8, d), lambda i, j: (i, jnp.minimum((j + 1) * n8, last8), 0)),
            pl.BlockSpec((1, tm, D_F), tile),
            pl.BlockSpec((1, tm, D_F), tile),
            mod_spec(MOD_SCALE),
            mod_spec(MOD_SHIFT),
            mod_spec(MOD_GATE),
            pl.BlockSpec(w_in.shape, const2),
            pl.BlockSpec(gfold.shape, lambda i, j: (0, 0, 0)),
            pl.BlockSpec(wo.shape, const2),
            pl.BlockSpec(conv_w.shape, const2),
            pl.BlockSpec(conv_b.shape, const2),
            pl.BlockSpec(ln_g.shape, const2),
            pl.BlockSpec(ln_b.shape, const2),
        ],
        out_specs=pl.BlockSpec((1, tm, d), tile),
        out_shape=jax.ShapeDtypeStruct((b, s, d), jnp.float32),
        compiler_params=pltpu.CompilerParams(
            dimension_semantics=("arbitrary", "arbitrary"), vmem_limit_bytes=VMEM_LIMIT),
        name="mixer_main",
    )(x, x, x, ur, ui, mod, mod, mod, w_in, gfold, wo, conv_w, conv_b, ln_g, ln_b)


def kernel(x_prompt, x_sample, c_prompt, c_sample, w_ada, b_ada, w_in, conv_w, conv_b,
           w_fmix, w_out, ln_g, ln_b):
    bf16 = jnp.bfloat16
    w_ada, b_ada, w_in, conv_w, conv_b = w_ada[0], b_ada[0], w_in[0], conv_w[0], conv_b[0]
    w_fmix, w_out, ln_g, ln_b = w_fmix[0], w_out[0], ln_g[0], ln_b[0]

    w_main, w_vf = _regroup_w_in(w_in)
    wo = w_out.astype(bf16)
    gfold = _fold_fmix(w_fmix)

    nb = c_prompt.shape[0]
    mod = _modulation(jnp.concatenate([c_prompt, c_sample], axis=0), w_ada, b_ada)
    mod = mod.reshape(mod.shape[0], 1, 3 * D_MODEL)

    outs = []
    for x, row0 in ((x_prompt, 0), (x_sample, nb)):
        vf = _vf_proj(x, mod, row0, w_vf, tm=VF_ROWS)
        ur, ui = _seq_dft(vf)
        outs.append(_main(x, ur, ui, mod, row0, w_main, gfold, wo, conv_w, conv_b.reshape(1, -1),
                          ln_g.reshape(1, -1), ln_b.reshape(1, -1), tm=MAIN_ROWS))
    return tuple(outs)
```

```python
import functools

import numpy as np
import jax
import jax.numpy as jnp
from jax import lax
from jax.experimental import pallas as pl
from jax.experimental.pallas import tpu as pltpu

D_MODEL = 1024
HEAD_DIM = 128
N_GROUPS = 8
D_A = 1024
D_F = 1024
ALPHA = float(2.0 ** 0.25)
LN_EPS = 1e-5

COL_BG, COL_CG, COL_VA, COL_GA, COL_VF, COL_GF = range(6)
MOD_SHIFT, MOD_SCALE, MOD_GATE = range(3)
VF_ROWS = 2048
MAIN_ROWS = 1024
OUT_ROWS = 256
COL_BLOCK = 256

CHUNK = 16
FFT_LANES = 256
VMEM_LIMIT = 56 * 1024 * 1024
FFT_BLOCK_CYCLES = 4096
E1_GROUP = 8


def _fft_constants(seq):
    r1 = seq // 256
    eye = np.eye(16)
    d = np.arange(16)
    pk = np.where(d[:, None] <= 8, np.cos(2 * np.pi * d[None, :] * d[:, None] / 16),
                  -np.sin(2 * np.pi * d[None, :] * (d[:, None] - 8) / 16))
    m1 = np.einsum('ab,pd->apdb', eye, pk).reshape(256, 256)

    unpack = np.zeros((16, 16), complex)
    unpack[0, 0] = unpack[8, 8] = 1
    for e0 in range(1, 8):
        unpack[e0, e0], unpack[e0, 8 + e0] = 1, 1j
        unpack[16 - e0, e0], unpack[16 - e0, 8 + e0] = 1, -1j
    dd = np.arange(r1)
    lo = np.arange(r1 // 2 + 1)
    ph = dd[None, None, :] * lo[:, None, None] / r1 + dd[None, None, :] * d[None, :, None] / (16.0 * r1)
    c = np.exp(-2j * np.pi * ph)
    m2c = np.einsum('aed,ep->aedp', c, unpack).reshape(16 * len(lo), 16 * r1)
    blk = E1_GROUP * 16
    m2 = np.concatenate([np.concatenate([m2c.real[i:i + blk], m2c.imag[i:i + blk]], axis=0)
                         for i in range(0, 16 * len(lo), blk)], axis=0)

    ph = (d[None, None, :] * d[:, None, None]) / 16.0 + d[None, None, :] * d[None, :, None] / float(seq)
    c = np.exp(-2j * np.pi * ph) / np.sqrt(seq)
    mirror = np.zeros((16, 16))
    mirror[(16 - d) % 16, d] = 1
    cf = np.einsum('ald,lm->aldm', c, eye).reshape(256, 256)
    cfu = np.einsum('ald,lm->aldm', c, mirror).reshape(256, 256)
    m3 = np.stack([cf.real, cf.imag, cfu.real, cfu.imag])

    tw = np.repeat(np.exp(-2j * np.pi * np.outer(dd, d) / (16.0 * r1)), 16, axis=1)
    f32 = np.float32
    return m1.astype(f32), m2.astype(f32), m3.astype(f32), tw.real.astype(f32), tw.imag.astype(f32)


def _channel_dft():
    n = np.arange(HEAD_DIM)
    ang = 2 * np.pi * np.outer(n, n) / HEAD_DIM
    cs = np.concatenate([np.cos(ang), np.sin(ang)], axis=0) / np.sqrt(HEAD_DIM)
    return cs.astype(np.float32)


def _mod_kernel(c_ref, w_ref, b_ref, o_ref):
    c = c_ref[...]
    o_ref[...] = jnp.dot(jax.nn.silu(c), w_ref[...], precision=lax.Precision.HIGHEST,
                         preferred_element_type=jnp.float32) + b_ref[...]


def _modulation(c_all, w_ada, b_ada):
    n, d = c_all.shape
    tn = 512
    return pl.pallas_call(
        _mod_kernel,
        grid=(3 * D_MODEL // tn,),
        in_specs=[
            pl.BlockSpec((n, d), lambda j: (0, 0)),
            pl.BlockSpec((d, tn), lambda j: (0, j)),
            pl.BlockSpec((1, tn), lambda j: (0, j)),
        ],
        out_specs=pl.BlockSpec((n, tn), lambda j: (0, j)),
        out_shape=jax.ShapeDtypeStruct((n, 3 * D_MODEL), jnp.float32),
        compiler_params=pltpu.CompilerParams(dimension_semantics=("arbitrary",)),
        name="adaln_mod",
    )(c_all, w_ada, b_ada.reshape(1, -1))


def _fold_kernel(cs_ref, w_ref, o_ref):
    hd = HEAD_DIM
    o_ref[...] = jnp.zeros(o_ref.shape, o_ref.dtype)
    for g in range(N_GROUPS):
        t = jnp.dot(cs_ref[...], w_ref[g], precision=lax.Precision.HIGHEST,
                    preferred_element_type=jnp.float32).astype(o_ref.dtype)
        p, k = divmod(g, 2)
        o_ref[p, k * hd:(k + 1) * hd, k * hd:(k + 1) * hd] = t[:hd]
        o_ref[p, (2 + k) * hd:(3 + k) * hd, k * hd:(k + 1) * hd] = t[hd:]


def _fold_fmix(w_fmix):
    cs = jnp.asarray(_channel_dft())
    return pl.pallas_call(
        _fold_kernel,
        out_shape=jax.ShapeDtypeStruct((N_GROUPS // 2, 4 * HEAD_DIM, 2 * HEAD_DIM), jnp.bfloat16),
        name="fold_fmix",
    )(cs, w_fmix)


def _regroup_kernel(cg_ref, va_ref, bg_ref, ga_ref, gf_ref, vf_ref, main_ref, vfo_ref):
    for k, ref in enumerate((cg_ref, va_ref, bg_ref, ga_ref, gf_ref)):
        main_ref[:, k * COL_BLOCK:(k + 1) * COL_BLOCK] = ref[...].astype(main_ref.dtype)
    vfo_ref[...] = vf_ref[...].astype(vfo_ref.dtype)


def _regroup_w_in(w_in):
    d = w_in.shape[0]
    n_blocks = D_A // COL_BLOCK
    src = lambda k: pl.BlockSpec((d, COL_BLOCK), lambda c: (0, k * n_blocks + c))
    return pl.pallas_call(
        _regroup_kernel,
        grid=(n_blocks,),
        in_specs=[src(k) for k in (COL_CG, COL_VA, COL_BG, COL_GA, COL_GF, COL_VF)],
        out_specs=[pl.BlockSpec((d, 5 * COL_BLOCK), lambda c: (0, c)),
                   pl.BlockSpec((d, COL_BLOCK), lambda c: (0, c))],
        out_shape=[jax.ShapeDtypeStruct((d, 5 * D_A), jnp.bfloat16),
                   jax.ShapeDtypeStruct((d, D_F), jnp.bfloat16)],
        compiler_params=pltpu.CompilerParams(dimension_semantics=("arbitrary",)),
        name="regroup_w_in",
    )(w_in, w_in, w_in, w_in, w_in, w_in)


def _vf_kernel(x_ref, scale_ref, shift_ref, w_ref, o_ref):
    h = x_ref[0] * (1.0 + scale_ref[0]) + shift_ref[0]
    o_ref[0] = jnp.dot(h.astype(jnp.bfloat16), w_ref[...],
                       preferred_element_type=jnp.float32).astype(o_ref.dtype)


def _vf_proj(x, mod, row0, w_in, tm):
    b, s, d = x.shape
    return pl.pallas_call(
        _vf_kernel,
        grid=(b, s // tm),
        in_specs=[
            pl.BlockSpec((1, tm, d), lambda i, j: (i, j, 0)),
            pl.BlockSpec((1, 1, d), lambda i, j: (row0 + i, 0, MOD_SCALE)),
            pl.BlockSpec((1, 1, d), lambda i, j: (row0 + i, 0, MOD_SHIFT)),
            pl.BlockSpec((d, D_F), lambda i, j: (0, 0)),
        ],
        out_specs=pl.BlockSpec((1, tm, D_F), lambda i, j: (i, j, 0)),
        out_shape=jax.ShapeDtypeStruct((b, s, D_F), jnp.bfloat16),
        compiler_params=pltpu.CompilerParams(
            dimension_semantics=("arbitrary", "arbitrary"), vmem_limit_bytes=VMEM_LIMIT),
        name="vf_proj",
    )(x, mod, mod, w_in)


def _unroll(trips, mxu_cycles):
    want = max(1, min(trips, FFT_BLOCK_CYCLES // mxu_cycles))
    return max(u for u in range(1, want + 1) if trips % u == 0)


def _fft_kernel(v_ref, m1_ref, m2_ref, m3_ref, twr_ref, twi_ref, ur_ref, ui_ref, y2_ref, m3v_ref, *, r1):
    k2 = CHUNK * r1
    half1 = r1 // 2
    bf16 = jnp.bfloat16

    @pl.when((pl.program_id(0) == 0) & (pl.program_id(1) == 0))
    def _():
        def fold(base, sign):
            def body(e1, carry):
                wr = twr_ref[pl.ds(e1, 1), :]
                wi = twi_ref[pl.ds(e1, 1), :]
                cr = m3_ref[base] * wr - m3_ref[base + 1] * wi
                ci = m3_ref[base] * wi + m3_ref[base + 1] * wr
                for p, val in enumerate((cr, ci, cr + sign * ci)):
                    m3v_ref[e1, p] = val.astype(bf16)
                return carry
            return body
        lax.fori_loop(0, half1, fold(0, 1.0), 0)
        lax.fori_loop(half1, r1, fold(2, -1.0), 0)

    def stage1(d1, carry):
        rhs = jnp.concatenate(
            [v_ref[0, pl.ds(pl.multiple_of(k2 * d2 + CHUNK * d1, CHUNK), CHUNK), :] for d2 in range(16)], axis=0)
        r = jnp.dot(m1_ref[...], rhs, preferred_element_type=jnp.float32)
        ur_ref[0, pl.ds(pl.multiple_of(256 * d1, 256), 256), :] = r.astype(bf16)
        return carry

    lax.fori_loop(0, r1, stage1, 0, unroll=_unroll(r1, 64))

    def stage2(d0, carry):
        off = pl.multiple_of(CHUNK * d0, CHUNK)
        rhs = jnp.concatenate([ur_ref[0, pl.ds(256 * d1 + off, CHUNK), :] for d1 in range(r1)], axis=0)
        row = 0
        for n_e1 in [E1_GROUP] * (half1 // E1_GROUP) + [1]:
            rows = CHUNK * n_e1
            r = jnp.dot(m2_ref[2 * row:2 * (row + rows), :], rhs, preferred_element_type=jnp.float32)
            yr, yi = r[:rows], r[rows:]
            for p, val in enumerate((yr, yi, yr + yi)):
                y2_ref[d0, p, row:row + rows, :] = val.astype(bf16)
            row += rows
        return carry

    lax.fori_loop(0, 16, stage2, 0, unroll=_unroll(16, r1 * r1 // 4 + 64))

    def store(e1, re, im):
        off = pl.multiple_of(CHUNK * e1, CHUNK)
        for e2 in range(16):
            row = pl.multiple_of(k2 * e2 + off, CHUNK)
            ur_ref[0, pl.ds(row, CHUNK), :] = re[CHUNK * e2:CHUNK * (e2 + 1)].astype(bf16)
            ui_ref[0, pl.ds(row, CHUNK), :] = im[CHUNK * e2:CHUNK * (e2 + 1)].astype(bf16)

    def stage3_low(e1, carry):
        off = pl.multiple_of(CHUNK * e1, CHUNK)
        t1, t2, t3 = (
            jnp.dot(m3v_ref[e1, p],
                    jnp.concatenate([y2_ref[d0, p, pl.ds(off, CHUNK), :] for d0 in range(16)], axis=0),
                    preferred_element_type=jnp.float32)
            for p in range(3))
        store(e1, t1 - t2, t3 - t1 - t2)
        return carry

    lax.fori_loop(0, half1, stage3_low, 0, unroll=_unroll(half1, 192))

    first_row = lax.broadcasted_iota(jnp.int32, (CHUNK, 1), 0) == 0

    def stage3_high(e1, carry):
        off_a = pl.multiple_of(CHUNK * (r1 - 1 - e1), CHUNK)
        off_b = pl.multiple_of(CHUNK * (r1 - e1), CHUNK)
        t1, t2, t3 = (
            jnp.dot(m3v_ref[e1, p],
                    jnp.concatenate([jnp.where(first_row, y2_ref[d0, p, pl.ds(off_b, CHUNK), :],
                                               y2_ref[d0, p, pl.ds(off_a, CHUNK), :]) for d0 in range(16)], axis=0),
                    preferred_element_type=jnp.float32)
            for p in range(3))
        store(e1, t1 + t2, t1 - t2 - t3)
        return carry

    lax.fori_loop(half1, r1, stage3_high, 0, unroll=_unroll(half1, 192))


def _seq_dft(v):
    b, s, c = v.shape
    r1 = s // 256
    m1, m2, m3, twr, twi = _fft_constants(s)
    m1 = jnp.asarray(m1).astype(jnp.bfloat16)
    m2 = jnp.asarray(m2).astype(jnp.bfloat16)
    blk = pl.BlockSpec((1, s, FFT_LANES), lambda i, j: (i, 0, j))
    const2 = lambda i, j: (0, 0)
    return pl.pallas_call(
        functools.partial(_fft_kernel, r1=r1),
        grid=(b, c // FFT_LANES),
        in_specs=[
            blk,
            pl.BlockSpec(m1.shape, const2),
            pl.BlockSpec(m2.shape, const2),
            pl.BlockSpec(m3.shape, lambda i, j: (0, 0, 0)),
            pl.BlockSpec(twr.shape, const2),
            pl.BlockSpec(twi.shape, const2),
        ],
        out_specs=[blk, blk],
        out_shape=[jax.ShapeDtypeStruct((b, s, c), jnp.bfloat16)] * 2,
        scratch_shapes=[pltpu.VMEM((16, 3, CHUNK * (r1 // 2 + 1), FFT_LANES), jnp.bfloat16),
                        pltpu.VMEM((r1, 3, 256, 256), jnp.bfloat16)],
        compiler_params=pltpu.CompilerParams(
            dimension_semantics=("arbitrary", "arbitrary"), vmem_limit_bytes=VMEM_LIMIT),
        name="seq_dft",
    )(v, m1, m2, jnp.asarray(m3), jnp.asarray(twr), jnp.asarray(twi))


def _main_kernel(x_ref, xp_ref, xn_ref, ur_ref, ui_ref, scale_ref, shift_ref, gate_ref,
                 w_ref, g_ref, wo_ref, cw_ref, cb_ref, lg_ref, lb_ref, o_ref, *, tm):
    i = pl.program_id(1)
    n_tiles = pl.num_programs(1)
    bf16 = jnp.bfloat16
    x = x_ref[0]
    s1 = 1.0 + scale_ref[0]
    sh = shift_ref[0]
    h = x * s1 + sh
    h_halo = jnp.concatenate([xp_ref[0], xn_ref[0]], axis=0) * s1 + sh
    h_ext = jnp.concatenate([h, h_halo], axis=0).astype(bf16)

    hb = h_ext[:tm]
    ur = ur_ref[0]
    ui = ui_ref[0]
    rows = lax.broadcasted_iota(jnp.int32, (tm, 1), 0)
    y_a, y_f = [], []
    for c in range(D_A // COL_BLOCK):
        lo, hi = c * COL_BLOCK, (c + 1) * COL_BLOCK
        w0 = c * 5 * COL_BLOCK
        zc = jnp.dot(h_ext, w_ref[:, w0:w0 + 2 * COL_BLOCK], preferred_element_type=jnp.float32)
        u_all = zc[:, :COL_BLOCK] * zc[:, COL_BLOCK:]
        u = u_all[:tm]
        u_prev = jnp.where(i > 0, u_all[tm + 7:tm + 8], 0.0)
        u_next = jnp.where(i < n_tiles - 1, u_all[tm + 8:tm + 9], 0.0)
        u_m1 = jnp.where(rows == 0, u_prev, pltpu.roll(u, 1, 0))
        u_p1 = jnp.where(rows == tm - 1, u_next, pltpu.roll(u, tm - 1, 0))
        conv = (u_m1 * cw_ref[0:1, lo:hi] + u * cw_ref[1:2, lo:hi] + u_p1 * cw_ref[2:3, lo:hi]
                + cb_ref[:, lo:hi])

        zb = jnp.dot(hb, w_ref[:, w0 + 2 * COL_BLOCK:w0 + 5 * COL_BLOCK], preferred_element_type=jnp.float32)
        bg, ga, gf = (zb[:, k * COL_BLOCK:(k + 1) * COL_BLOCK] for k in range(3))
        y_a.append((bg * conv * jax.nn.silu(ga)).astype(bf16))

        lhs = jnp.concatenate([ur[:, lo:hi], ui[:, lo:hi]], axis=1)
        fr = jnp.dot(lhs, g_ref[c], preferred_element_type=jnp.float32)
        y_f.append((fr * jax.nn.silu(gf)).astype(bf16))

    y = jnp.concatenate(y_a + y_f, axis=1)
    for m in range(0, tm, OUT_ROWS):
        o = jnp.dot(y[m:m + OUT_ROWS], wo_ref[...], preferred_element_type=jnp.float32)
        r = ALPHA * x[m:m + OUT_ROWS] + gate_ref[0] * o
        mu = jnp.mean(r, axis=-1, keepdims=True)
        rc = r - mu
        var = jnp.mean(rc * rc, axis=-1, keepdims=True)
        o_ref[0, m:m + OUT_ROWS, :] = rc * lax.rsqrt(var + LN_EPS) * lg_ref[...] + lb_ref[...]


def _main(x, ur, ui, mod, row0, w_in, gfold, wo, conv_w, conv_b, ln_g, ln_b, tm):
    b, s, d = x.shape
    n8 = tm // 8
    last8 = s // 8 - 1
    tile = lambda i, j: (i, j, 0)
    const2 = lambda i, j: (0, 0)
    mod_spec = lambda k: pl.BlockSpec((1, 1, d), lambda i, j: (row0 + i, 0, k))
    return pl.pallas_call(
        functools.partial(_main_kernel, tm=tm),
        grid=(b, s // tm),
        in_specs=[
            pl.BlockSpec((1, tm, d), tile),
            pl.BlockSpec((1, 8, d), lambda i, j: (i, jnp.maximum(j * n8 - 1, 0), 0)),
            pl.BlockSpec((1, 8, d), lambda i, j: (i, jnp.minimum((j + 1) * n8, last8), 0)),
            pl.BlockSpec((1, tm, D_F), tile),
            pl.BlockSpec((1, tm, D_F), tile),
            mod_spec(MOD_SCALE),
            mod_spec(MOD_SHIFT),
            mod_spec(MOD_GATE),
            pl.BlockSpec(w_in.shape, const2),
            pl.BlockSpec(gfold.shape, lambda i, j: (0, 0, 0)),
            pl.BlockSpec(wo.shape, const2),
            pl.BlockSpec(conv_w.shape, const2),
            pl.BlockSpec(conv_b.shape, const2),
            pl.BlockSpec(ln_g.shape, const2),
            pl.BlockSpec(ln_b.shape, const2),
        ],
        out_specs=pl.BlockSpec((1, tm, d), tile),
        out_shape=jax.ShapeDtypeStruct((b, s, d), jnp.float32),
        compiler_params=pltpu.CompilerParams(
            dimension_semantics=("arbitrary", "arbitrary"), vmem_limit_bytes=VMEM_LIMIT),
        name="mixer_main",
    )(x, x, x, ur, ui, mod, mod, mod, w_in, gfold, wo, conv_w, conv_b, ln_g, ln_b)


def kernel(x_prompt, x_sample, c_prompt, c_sample, w_ada, b_ada, w_in, conv_w, conv_b,
           w_fmix, w_out, ln_g, ln_b):
    bf16 = jnp.bfloat16
    w_ada, b_ada, w_in, conv_w, conv_b = w_ada[0], b_ada[0], w_in[0], conv_w[0], conv_b[0]
    w_fmix, w_out, ln_g, ln_b = w_fmix[0], w_out[0], ln_g[0], ln_b[0]

    w_main, w_vf = _regroup_w_in(w_in)
    wo = w_out.astype(bf16)
    gfold = _fold_fmix(w_fmix)

    nb = c_prompt.shape[0]
    mod = _modulation(jnp.concatenate([c_prompt, c_sample], axis=0), w_ada, b_ada)
    mod = mod.reshape(mod.shape[0], 1, 3 * D_MODEL)

    outs = []
    for x, row0 in ((x_prompt, 0), (x_sample, nb)):
        vf = _vf_proj(x, mod, row0, w_vf, tm=VF_ROWS)
        ur, ui = _seq_dft(vf)
        outs.append(_main(x, ur, ui, mod, row0, w_main, gfold, wo, conv_w, conv_b.reshape(1, -1),
                          ln_g.reshape(1, -1), ln_b.reshape(1, -1), tm=MAIN_ROWS))
    return tuple(outs)
```

```python
import functools

import numpy as np
import jax
import jax.numpy as jnp
from jax import lax
from jax.experimental import pallas as pl
from jax.experimental.pallas import tpu as pltpu

D_MODEL = 1024
HEAD_DIM = 128
N_GROUPS = 8
D_A = 1024
D_F = 1024
ALPHA = float(2.0 ** 0.25)
LN_EPS = 1e-5

COL_BG, COL_CG, COL_VA, COL_GA, COL_VF, COL_GF = range(6)
MOD_SHIFT, MOD_SCALE, MOD_GATE = range(3)
VF_ROWS = 2048
MAIN_ROWS = 1024
OUT_ROWS = 256
COL_BLOCK = 256

CHUNK = 16
FFT_LANES = 256
VMEM_LIMIT = 56 * 1024 * 1024
FFT_BLOCK_CYCLES = 4096
E1_GROUP = 8


def _fft_constants(seq):
    r1 = seq // 256
    eye = np.eye(16)
    d = np.arange(16)
    pk = np.where(d[:, None] <= 8, np.cos(2 * np.pi * d[None, :] * d[:, None] / 16),
                  -np.sin(2 * np.pi * d[None, :] * (d[:, None] - 8) / 16))
    m1 = np.einsum('ab,pd->apdb', eye, pk).reshape(256, 256)

    unpack = np.zeros((16, 16), complex)
    unpack[0, 0] = unpack[8, 8] = 1
    for e0 in range(1, 8):
        unpack[e0, e0], unpack[e0, 8 + e0] = 1, 1j
        unpack[16 - e0, e0], unpack[16 - e0, 8 + e0] = 1, -1j
    dd = np.arange(r1)
    lo = np.arange(r1 // 2 + 1)
    ph = dd[None, None, :] * lo[:, None, None] / r1 + dd[None, None, :] * d[None, :, None] / (16.0 * r1)
    c = np.exp(-2j * np.pi * ph)
    m2c = np.einsum('aed,ep->aedp', c, unpack).reshape(16 * len(lo), 16 * r1)
    blk = E1_GROUP * 16
    m2 = np.concatenate([np.concatenate([m2c.real[i:i + blk], m2c.imag[i:i + blk]], axis=0)
                         for i in range(0, 16 * len(lo), blk)], axis=0)

    ph = (d[None, None, :] * d[:, None, None]) / 16.0 + d[None, None, :] * d[None, :, None] / float(seq)
    c = np.exp(-2j * np.pi * ph) / np.sqrt(seq)
    mirror = np.zeros((16, 16))
    mirror[(16 - d) % 16, d] = 1
    cf = np.einsum('ald,lm->aldm', c, eye).reshape(256, 256)
    cfu = np.einsum('ald,lm->aldm', c, mirror).reshape(256, 256)
    m3 = np.stack([cf.real, cf.imag, cfu.real, cfu.imag])

    tw = np.repeat(np.exp(-2j * np.pi * np.outer(dd, d) / (16.0 * r1)), 16, axis=1)
    f32 = np.float32
    return m1.astype(f32), m2.astype(f32), m3.astype(f32), tw.real.astype(f32), tw.imag.astype(f32)


def _channel_dft():
    n = np.arange(HEAD_DIM)
    ang = 2 * np.pi * np.outer(n, n) / HEAD_DIM
    cs = np.concatenate([np.cos(ang), np.sin(ang)], axis=0) / np.sqrt(HEAD_DIM)
    return cs.astype(np.float32)


def _mod_kernel(c_ref, w_ref, b_ref, o_ref):
    c = c_ref[...]
    o_ref[...] = jnp.dot(jax.nn.silu(c), w_ref[...], precision=lax.Precision.HIGHEST,
                         preferred_element_type=jnp.float32) + b_ref[...]


def _modulation(c_all, w_ada, b_ada):
    n, d = c_all.shape
    tn = 512
    return pl.pallas_call(
        _mod_kernel,
        grid=(3 * D_MODEL // tn,),
        in_specs=[
            pl.BlockSpec((n, d), lambda j: (0, 0)),
            pl.BlockSpec((d, tn), lambda j: (0, j)),
            pl.BlockSpec((1, tn), lambda j: (0, j)),
        ],
        out_specs=pl.BlockSpec((n, tn), lambda j: (0, j)),
        out_shape=jax.ShapeDtypeStruct((n, 3 * D_MODEL), jnp.float32),
        compiler_params=pltpu.CompilerParams(dimension_semantics=("arbitrary",)),
        name="adaln_mod",
    )(c_all, w_ada, b_ada.reshape(1, -1))


def _fold_kernel(cs_ref, w_ref, o_ref):
    hd = HEAD_DIM
    o_ref[...] = jnp.zeros(o_ref.shape, o_ref.dtype)
    for g in range(N_GROUPS):
        t = jnp.dot(cs_ref[...], w_ref[g], precision=lax.Precision.HIGHEST,
                    preferred_element_type=jnp.float32).astype(o_ref.dtype)
        p, k = divmod(g, 2)
        o_ref[p, k * hd:(k + 1) * hd, k * hd:(k + 1) * hd] = t[:hd]
        o_ref[p, (2 + k) * hd:(3 + k) * hd, k * hd:(k + 1) * hd] = t[hd:]


def _fold_fmix(w_fmix):
    cs = jnp.asarray(_channel_dft())
    return pl.pallas_call(
        _fold_kernel,
        out_shape=jax.ShapeDtypeStruct((N_GROUPS // 2, 4 * HEAD_DIM, 2 * HEAD_DIM), jnp.bfloat16),
        name="fold_fmix",
    )(cs, w_fmix)


def _regroup_kernel(cg_ref, va_ref, bg_ref, ga_ref, gf_ref, vf_ref, main_ref, vfo_ref):
    for k, ref in enumerate((cg_ref, va_ref, bg_ref, ga_ref, gf_ref)):
        main_ref[:, k * COL_BLOCK:(k + 1) * COL_BLOCK] = ref[...].astype(main_ref.dtype)
    vfo_ref[...] = vf_ref[...].astype(vfo_ref.dtype)


def _regroup_w_in(w_in):
    d = w_in.shape[0]
    n_blocks = D_A // COL_BLOCK
    src = lambda k: pl.BlockSpec((d, COL_BLOCK), lambda c: (0, k * n_blocks + c))
    return pl.pallas_call(
        _regroup_kernel,
        grid=(n_blocks,),
        in_specs=[src(k) for k in (COL_CG, COL_VA, COL_BG, COL_GA, COL_GF, COL_VF)],
        out_specs=[pl.BlockSpec((d, 5 * COL_BLOCK), lambda c: (0, c)),
                   pl.BlockSpec((d, COL_BLOCK), lambda c: (0, c))],
        out_shape=[jax.ShapeDtypeStruct((d, 5 * D_A), jnp.bfloat16),
                   jax.ShapeDtypeStruct((d, D_F), jnp.bfloat16)],
        compiler_params=pltpu.CompilerParams(dimension_semantics=("arbitrary",)),
        name="regroup_w_in",
    )(w_in, w_in, w_in, w_in, w_in, w_in)


def _vf_kernel(x_ref, scale_ref, shift_ref, w_ref, o_ref):
    h = x_ref[0] * (1.0 + scale_ref[0]) + shift_ref[0]
    o_ref[0] = jnp.dot(h.astype(jnp.bfloat16), w_ref[...],
                       preferred_element_type=jnp.float32).astype(o_ref.dtype)


def _vf_proj(x, mod, row0, w_in, tm):
    b, s, d = x.shape
    return pl.pallas_call(
        _vf_kernel,
        grid=(b, s // tm),
        in_specs=[
            pl.BlockSpec((1, tm, d), lambda i, j: (i, j, 0)),
            pl.BlockSpec((1, 1, d), lambda i, j: (row0 + i, 0, MOD_SCALE)),
            pl.BlockSpec((1, 1, d), lambda i, j: (row0 + i, 0, MOD_SHIFT)),
            pl.BlockSpec((d, D_F), lambda i, j: (0, 0)),
        ],
        out_specs=pl.BlockSpec((1, tm, D_F), lambda i, j: (i, j, 0)),
        out_shape=jax.ShapeDtypeStruct((b, s, D_F), jnp.bfloat16),
        compiler_params=pltpu.CompilerParams(
            dimension_semantics=("arbitrary", "arbitrary"), vmem_limit_bytes=VMEM_LIMIT),
        name="vf_proj",
    )(x, mod, mod, w_in)


def _unroll(trips, mxu_cycles):
    want = max(1, min(trips, FFT_BLOCK_CYCLES // mxu_cycles))
    return max(u for u in range(1, want + 1) if trips % u == 0)


def _fft_kernel(v_ref, m1_ref, m2_ref, m3_ref, twr_ref, twi_ref, ur_ref, ui_ref, y2_ref, m3v_ref, *, r1):
    k2 = CHUNK * r1
    half1 = r1 // 2
    bf16 = jnp.bfloat16

    @pl.when((pl.program_id(0) == 0) & (pl.program_id(1) == 0))
    def _():
        def fold(base, sign):
            def body(e1, carry):
                wr = twr_ref[pl.ds(e1, 1), :]
                wi = twi_ref[pl.ds(e1, 1), :]
                cr = m3_ref[base] * wr - m3_ref[base + 1] * wi
                ci = m3_ref[base] * wi + m3_ref[base + 1] * wr
                for p, val in enumerate((cr, ci, cr + sign * ci)):
                    m3v_ref[e1, p] = val.astype(bf16)
                return carry
            return body
        lax.fori_loop(0, half1, fold(0, 1.0), 0)
        lax.fori_loop(half1, r1, fold(2, -1.0), 0)

    def stage1(d1, carry):
        rhs = jnp.concatenate(
            [v_ref[0, pl.ds(pl.multiple_of(k2 * d2 + CHUNK * d1, CHUNK), CHUNK), :] for d2 in range(16)], axis=0)
        r = jnp.dot(m1_ref[...], rhs, preferred_element_type=jnp.float32)
        ur_ref[0, pl.ds(pl.multiple_of(256 * d1, 256), 256), :] = r.astype(bf16)
        return carry

    lax.fori_loop(0, r1, stage1, 0, unroll=_unroll(r1, 64))

    def stage2(d0, carry):
        off = pl.multiple_of(CHUNK * d0, CHUNK)
        rhs = jnp.concatenate([ur_ref[0, pl.ds(256 * d1 + off, CHUNK), :] for d1 in range(r1)], axis=0)
        row = 0
        for n_e1 in [E1_GROUP] * (half1 // E1_GROUP) + [1]:
            rows = CHUNK * n_e1
            r = jnp.dot(m2_ref[2 * row:2 * (row + rows), :], rhs, preferred_element_type=jnp.float32)
            yr, yi = r[:rows], r[rows:]
            for p, val in enumerate((yr, yi, yr + yi)):
                y2_ref[d0, p, row:row + rows, :] = val.astype(bf16)
            row += rows
        return carry

    lax.fori_loop(0, 16, stage2, 0, unroll=_unroll(16, r1 * r1 // 4 + 64))

    def store(e1, re, im):
        off = pl.multiple_of(CHUNK * e1, CHUNK)
        for e2 in range(16):
            row = pl.multiple_of(k2 * e2 + off, CHUNK)
            ur_ref[0, pl.ds(row, CHUNK), :] = re[CHUNK * e2:CHUNK * (e2 + 1)].astype(bf16)
            ui_ref[0, pl.ds(row, CHUNK), :] = im[CHUNK * e2:CHUNK * (e2 + 1)].astype(bf16)

    def stage3_low(e1, carry):
        off = pl.multiple_of(CHUNK * e1, CHUNK)
        t1, t2, t3 = (
            jnp.dot(m3v_ref[e1, p],
                    jnp.concatenate([y2_ref[d0, p, pl.ds(off, CHUNK), :] for d0 in range(16)], axis=0),
                    preferred_element_type=jnp.float32)
            for p in range(3))
        store(e1, t1 - t2, t3 - t1 - t2)
        return carry

    lax.fori_loop(0, half1, stage3_low, 0, unroll=_unroll(half1, 192))

    first_row = lax.broadcasted_iota(jnp.int32, (CHUNK, 1), 0) == 0

    def stage3_high(e1, carry):
        off_a = pl.multiple_of(CHUNK * (r1 - 1 - e1), CHUNK)
        off_b = pl.multiple_of(CHUNK * (r1 - e1), CHUNK)
        t1, t2, t3 = (
            jnp.dot(m3v_ref[e1, p],
                    jnp.concatenate([jnp.where(first_row, y2_ref[d0, p, pl.ds(off_b, CHUNK), :],
                                               y2_ref[d0, p, pl.ds(off_a, CHUNK), :]) for d0 in range(16)], axis=0),
                    preferred_element_type=jnp.float32)
            for p in range(3))
        store(e1, t1 + t2, t1 - t2 - t3)
        return carry

    lax.fori_loop(half1, r1, stage3_high, 0, unroll=_unroll(half1, 192))


def _seq_dft(v):
    b, s, c = v.shape
    r1 = s // 256
    m1, m2, m3, twr, twi = _fft_constants(s)
    m1 = jnp.asarray(m1).astype(jnp.bfloat16)
    m2 = jnp.asarray(m2).astype(jnp.bfloat16)
    blk = pl.BlockSpec((1, s, FFT_LANES), lambda i, j: (i, 0, j))
    const2 = lambda i, j: (0, 0)
    return pl.pallas_call(
        functools.partial(_fft_kernel, r1=r1),
        grid=(b, c // FFT_LANES),
        in_specs=[
            blk,
            pl.BlockSpec(m1.shape, const2),
            pl.BlockSpec(m2.shape, const2),
            pl.BlockSpec(m3.shape, lambda i, j: (0, 0, 0)),
            pl.BlockSpec(twr.shape, const2),
            pl.BlockSpec(twi.shape, const2),
        ],
        out_specs=[blk, blk],
        out_shape=[jax.ShapeDtypeStruct((b, s, c), jnp.bfloat16)] * 2,
        scratch_shapes=[pltpu.VMEM((16, 3, CHUNK * (r1 // 2 + 1), FFT_LANES), jnp.bfloat16),
                        pltpu.VMEM((r1, 3, 256, 256), jnp.bfloat16)],
        compiler_params=pltpu.CompilerParams(
            dimension_semantics=("arbitrary", "arbitrary"), vmem_limit_bytes=VMEM_LIMIT),
        name="seq_dft",
    )(v, m1, m2, jnp.asarray(m3), jnp.asarray(twr), jnp.asarray(twi))


def _main_kernel(x_ref, xp_ref, xn_ref, ur_ref, ui_ref, scale_ref, shift_ref, gate_ref,
                 w_ref, g_ref, wo_ref, cw_ref, cb_ref, lg_ref, lb_ref, o_ref, *, tm):
    i = pl.program_id(1)
    n_tiles = pl.num_programs(1)
    bf16 = jnp.bfloat16
    x = x_ref[0]
    s1 = 1.0 + scale_ref[0]
    sh = shift_ref[0]
    h = x * s1 + sh
    h_halo = jnp.concatenate([xp_ref[0], xn_ref[0]], axis=0) * s1 + sh
    h_ext = jnp.concatenate([h, h_halo], axis=0).astype(bf16)

    hb = h_ext[:tm]
    ur = ur_ref[0]
    ui = ui_ref[0]
    rows = lax.broadcasted_iota(jnp.int32, (tm, 1), 0)
    y_a, y_f = [], []
    for c in range(D_A // COL_BLOCK):
        lo, hi = c * COL_BLOCK, (c + 1) * COL_BLOCK
        w0 = c * 5 * COL_BLOCK
        zc = jnp.dot(h_ext, w_ref[:, w0:w0 + 2 * COL_BLOCK], preferred_element_type=jnp.float32)
        u_all = zc[:, :COL_BLOCK] * zc[:, COL_BLOCK:]
        u = u_all[:tm]
        u_prev = jnp.where(i > 0, u_all[tm + 7:tm + 8], 0.0)
        u_next = jnp.where(i < n_tiles - 1, u_all[tm + 8:tm + 9], 0.0)
        u_m1 = jnp.where(rows == 0, u_prev, pltpu.roll(u, 1, 0))
        u_p1 = jnp.where(rows == tm - 1, u_next, pltpu.roll(u, tm - 1, 0))
        conv = (u_m1 * cw_ref[0:1, lo:hi] + u * cw_ref[1:2, lo:hi] + u_p1 * cw_ref[2:3, lo:hi]
                + cb_ref[:, lo:hi])

        zb = jnp.dot(hb, w_ref[:, w0 + 2 * COL_BLOCK:w0 + 5 * COL_BLOCK], preferred_element_type=jnp.float32)
        bg, ga, gf = (zb[:, k * COL_BLOCK:(k + 1) * COL_BLOCK] for k in range(3))
        y_a.append((bg * conv * jax.nn.silu(ga)).astype(bf16))

        lhs = jnp.concatenate([ur[:, lo:hi], ui[:, lo:hi]], axis=1)
        fr = jnp.dot(lhs, g_ref[c], preferred_element_type=jnp.float32)
        y_f.append((fr * jax.nn.silu(gf)).astype(bf16))

    y = jnp.concatenate(y_a + y_f, axis=1)
    for m in range(0, tm, OUT_ROWS):
        o = jnp.dot(y[m:m + OUT_ROWS], wo_ref[...], preferred_element_type=jnp.float32)
        r = ALPHA * x[m:m + OUT_ROWS] + gate_ref[0] * o
        mu = jnp.mean(r, axis=-1, keepdims=True)
        rc = r - mu
        var = jnp.mean(rc * rc, axis=-1, keepdims=True)
        o_ref[0, m:m + OUT_ROWS, :] = rc * lax.rsqrt(var + LN_EPS) * lg_ref[...] + lb_ref[...]


def _main(x, ur, ui, mod, row0, w_in, gfold, wo, conv_w, conv_b, ln_g, ln_b, tm):
    b, s, d = x.shape
    n8 = tm // 8
    last8 = s // 8 - 1
    tile = lambda i, j: (i, j, 0)
    const2 = lambda i, j: (0, 0)
    mod_spec = lambda k: pl.BlockSpec((1, 1, d), lambda i, j: (row0 + i, 0, k))
    return pl.pallas_call(
        functools.partial(_main_kernel, tm=tm),
        grid=(b, s // tm),
        in_specs=[
            pl.BlockSpec((1, tm, d), tile),
            pl.BlockSpec((1, 8, d), lambda i, j: (i, jnp.maximum(j * n8 - 1, 0), 0)),
            pl.BlockSpec((1, 8, d), lambda i, j: (i, jnp.minimum((j + 1) * n8, last8), 0)),
            pl.BlockSpec((1, tm, D_F), tile),
            pl.BlockSpec((1, tm, D_F), tile),
            mod_spec(MOD_SCALE),
            mod_spec(MOD_SHIFT),
            mod_spec(MOD_GATE),
            pl.BlockSpec(w_in.shape, const2),
            pl.BlockSpec(gfold.shape, lambda i, j: (0, 0, 0)),
            pl.BlockSpec(wo.shape, const2),
            pl.BlockSpec(conv_w.shape, const2),
            pl.BlockSpec(conv_b.shape, const2),
            pl.BlockSpec(ln_g.shape, const2),
            pl.BlockSpec(ln_b.shape, const2),
        ],
        out_specs=pl.BlockSpec((1, tm, d), tile),
        out_shape=jax.ShapeDtypeStruct((b, s, d), jnp.float32),
        compiler_params=pltpu.CompilerParams(
            dimension_semantics=("arbitrary", "arbitrary"), vmem_limit_bytes=VMEM_LIMIT),
        name="mixer_main",
    )(x, x, x, ur, ui, mod, mod, mod, w_in, gfold, wo, conv_w, conv_b, ln_g, ln_b)


def kernel(x_prompt, x_sample, c_prompt, c_sample, w_ada, b_ada, w_in, conv_w, conv_b,
           w_fmix, w_out, ln_g, ln_b):
    bf16 = jnp.bfloat16
    assert w_in.shape == (1, D_MODEL, 4 * D_A + 2 * D_F) and w_out.shape == (1, D_A + D_F, D_MODEL)
    for x in (x_prompt, x_sample):
        seq = x.shape[1]
        assert x.shape[2] == D_MODEL and seq // 256 in (16, 32) and seq % 256 == 0, x.shape
        assert seq % VF_ROWS == 0 and seq % MAIN_ROWS == 0, x.shape
    w_ada, b_ada, w_in, conv_w, conv_b = w_ada[0], b_ada[0], w_in[0], conv_w[0], conv_b[0]
    w_fmix, w_out, ln_g, ln_b = w_fmix[0], w_out[0], ln_g[0], ln_b[0]

    w_main, w_vf = _regroup_w_in(w_in)
    wo = w_out.astype(bf16)
    gfold = _fold_fmix(w_fmix)

    nb = c_prompt.shape[0]
    mod = _modulation(jnp.concatenate([c_prompt, c_sample], axis=0), w_ada, b_ada)
    mod = mod.reshape(mod.shape[0], 1, 3 * D_MODEL)

    outs = []
    for x, row0 in ((x_prompt, 0), (x_sample, nb)):
        vf = _vf_proj(x, mod, row0, w_vf, tm=VF_ROWS)
        ur, ui = _seq_dft(vf)
        outs.append(_main(x, ur, ui, mod, row0, w_main, gfold, wo, conv_w, conv_b.reshape(1, -1),
                          ln_g.reshape(1, -1), ln_b.reshape(1, -1), tm=MAIN_ROWS))
    return tuple(outs)
```

```python
import functools

import numpy as np
import jax
import jax.numpy as jnp
from jax import lax
from jax.experimental import pallas as pl
from jax.experimental.pallas import tpu as pltpu

D_MODEL = 1024
HEAD_DIM = 128
N_GROUPS = 8
D_A = 1024
D_F = 1024
ALPHA = float(2.0 ** 0.25)
LN_EPS = 1e-5

COL_BG, COL_CG, COL_VA, COL_GA, COL_VF, COL_GF = range(6)
MOD_SHIFT, MOD_SCALE, MOD_GATE = range(3)
VF_ROWS = 2048
MAIN_ROWS = 1024
OUT_ROWS = 256
COL_BLOCK = 256

CHUNK = 16
FFT_LANES = 256
VMEM_LIMIT = 56 * 1024 * 1024
FFT_BLOCK_CYCLES = 4096
E1_GROUP = 8


def _fft_constants(seq):
    r1 = seq // 256
    eye = np.eye(16)
    d = np.arange(16)
    pk = np.where(d[:, None] <= 8, np.cos(2 * np.pi * d[None, :] * d[:, None] / 16),
                  -np.sin(2 * np.pi * d[None, :] * (d[:, None] - 8) / 16))
    m1 = np.einsum('ab,pd->apdb', eye, pk).reshape(256, 256)

    unpack = np.zeros((16, 16), complex)
    unpack[0, 0] = unpack[8, 8] = 1
    for e0 in range(1, 8):
        unpack[e0, e0], unpack[e0, 8 + e0] = 1, 1j
        unpack[16 - e0, e0], unpack[16 - e0, 8 + e0] = 1, -1j
    dd = np.arange(r1)
    lo = np.arange(r1 // 2 + 1)
    ph = dd[None, None, :] * lo[:, None, None] / r1 + dd[None, None, :] * d[None, :, None] / (16.0 * r1)
    c = np.exp(-2j * np.pi * ph)
    m2c = np.einsum('aed,ep->aedp', c, unpack).reshape(16 * len(lo), 16 * r1)
    blk = E1_GROUP * 16
    m2 = np.concatenate([np.concatenate([m2c.real[i:i + blk], m2c.imag[i:i + blk]], axis=0)
                         for i in range(0, 16 * len(lo), blk)], axis=0)

    ph = (d[None, None, :] * d[:, None, None]) / 16.0 + d[None, None, :] * d[None, :, None] / float(seq)
    c = np.exp(-2j * np.pi * ph) / np.sqrt(seq)
    mirror = np.zeros((16, 16))
    mirror[(16 - d) % 16, d] = 1
    cf = np.einsum('ald,lm->aldm', c, eye).reshape(256, 256)
    cfu = np.einsum('ald,lm->aldm', c, mirror).reshape(256, 256)
    m3 = np.stack([cf.real, cf.imag, cfu.real, cfu.imag])

    tw = np.repeat(np.exp(-2j * np.pi * np.outer(dd, d) / (16.0 * r1)), 16, axis=1)
    f32 = np.float32
    return m1.astype(f32), m2.astype(f32), m3.astype(f32), tw.real.astype(f32), tw.imag.astype(f32)


def _channel_dft():
    n = np.arange(HEAD_DIM)
    ang = 2 * np.pi * np.outer(n, n) / HEAD_DIM
    cs = np.concatenate([np.cos(ang), np.sin(ang)], axis=0) / np.sqrt(HEAD_DIM)
    return cs.astype(np.float32)


def _mod_kernel(c_ref, w_ref, b_ref, o_ref):
    c = c_ref[...]
    o_ref[...] = jnp.dot(jax.nn.silu(c), w_ref[...], precision=lax.Precision.HIGHEST,
                         preferred_element_type=jnp.float32) + b_ref[...]


def _modulation(c_all, w_ada, b_ada):
    n, d = c_all.shape
    tn = 512
    return pl.pallas_call(
        _mod_kernel,
        grid=(3 * D_MODEL // tn,),
        in_specs=[
            pl.BlockSpec((n, d), lambda j: (0, 0)),
            pl.BlockSpec((d, tn), lambda j: (0, j)),
            pl.BlockSpec((1, tn), lambda j: (0, j)),
        ],
        out_specs=pl.BlockSpec((n, tn), lambda j: (0, j)),
        out_shape=jax.ShapeDtypeStruct((n, 3 * D_MODEL), jnp.float32),
        compiler_params=pltpu.CompilerParams(dimension_semantics=("arbitrary",)),
        name="adaln_mod",
    )(c_all, w_ada, b_ada.reshape(1, -1))


def _fold_kernel(cs_ref, w_ref, o_ref):
    hd = HEAD_DIM
    o_ref[...] = jnp.zeros(o_ref.shape, o_ref.dtype)
    for g in range(N_GROUPS):
        t = jnp.dot(cs_ref[...], w_ref[g], precision=lax.Precision.HIGHEST,
                    preferred_element_type=jnp.float32).astype(o_ref.dtype)
        p, k = divmod(g, 2)
        o_ref[p, k * hd:(k + 1) * hd, k * hd:(k + 1) * hd] = t[:hd]
        o_ref[p, (2 + k) * hd:(3 + k) * hd, k * hd:(k + 1) * hd] = t[hd:]


def _fold_fmix(w_fmix):
    cs = jnp.asarray(_channel_dft())
    return pl.pallas_call(
        _fold_kernel,
        out_shape=jax.ShapeDtypeStruct((N_GROUPS // 2, 4 * HEAD_DIM, 2 * HEAD_DIM), jnp.bfloat16),
        name="fold_fmix",
    )(cs, w_fmix)


def _regroup_kernel(cg_ref, va_ref, bg_ref, ga_ref, gf_ref, vf_ref, main_ref, vfo_ref):
    for k, ref in enumerate((cg_ref, va_ref, bg_ref, ga_ref, gf_ref)):
        main_ref[:, k * COL_BLOCK:(k + 1) * COL_BLOCK] = ref[...].astype(main_ref.dtype)
    vfo_ref[...] = vf_ref[...].astype(vfo_ref.dtype)


def _regroup_w_in(w_in):
    d = w_in.shape[0]
    n_blocks = D_A // COL_BLOCK
    src = lambda k: pl.BlockSpec((d, COL_BLOCK), lambda c: (0, k * n_blocks + c))
    return pl.pallas_call(
        _regroup_kernel,
        grid=(n_blocks,),
        in_specs=[src(k) for k in (COL_CG, COL_VA, COL_BG, COL_GA, COL_GF, COL_VF)],
        out_specs=[pl.BlockSpec((d, 5 * COL_BLOCK), lambda c: (0, c)),
                   pl.BlockSpec((d, COL_BLOCK), lambda c: (0, c))],
        out_shape=[jax.ShapeDtypeStruct((d, 5 * D_A), jnp.bfloat16),
                   jax.ShapeDtypeStruct((d, D_F), jnp.bfloat16)],
        compiler_params=pltpu.CompilerParams(dimension_semantics=("arbitrary",)),
        name="regroup_w_in",
    )(w_in, w_in, w_in, w_in, w_in, w_in)


def _vf_kernel(x_ref, scale_ref, shift_ref, w_ref, o_ref):
    h = x_ref[0] * (1.0 + scale_ref[0]) + shift_ref[0]
    o_ref[0] = jnp.dot(h.astype(jnp.bfloat16), w_ref[...],
                       preferred_element_type=jnp.float32).astype(o_ref.dtype)


def _vf_proj(x, mod, row0, w_in, tm):
    b, s, d = x.shape
    return pl.pallas_call(
        _vf_kernel,
        grid=(b, s // tm),
        in_specs=[
            pl.BlockSpec((1, tm, d), lambda i, j: (i, j, 0)),
            pl.BlockSpec((1, 1, d), lambda i, j: (row0 + i, 0, MOD_SCALE)),
            pl.BlockSpec((1, 1, d), lambda i, j: (row0 + i, 0, MOD_SHIFT)),
            pl.BlockSpec((d, D_F), lambda i, j: (0, 0)),
        ],
        out_specs=pl.BlockSpec((1, tm, D_F), lambda i, j: (i, j, 0)),
        out_shape=jax.ShapeDtypeStruct((b, s, D_F), jnp.bfloat16),
        compiler_params=pltpu.CompilerParams(
            dimension_semantics=("arbitrary", "arbitrary"), vmem_limit_bytes=VMEM_LIMIT),
        name="vf_proj",
    )(x, mod, mod, w_in)


def _unroll(trips, mxu_cycles):
    want = max(1, min(trips, FFT_BLOCK_CYCLES // mxu_cycles))
    return max(u for u in range(1, want + 1) if trips % u == 0)


def _fft_kernel(v_ref, m1_ref, m2_ref, m3_ref, twr_ref, twi_ref, *rest, r1, side):
    if side:
        xs_ref, scale_ref, shift_ref, wv_ref, ur_ref, ui_ref, vfs_ref, y2_ref, m3v_ref = rest
        _vf_kernel(xs_ref, scale_ref, shift_ref, wv_ref, vfs_ref)
    else:
        ur_ref, ui_ref, y2_ref, m3v_ref = rest
    k2 = CHUNK * r1
    half1 = r1 // 2
    bf16 = jnp.bfloat16

    @pl.when((pl.program_id(0) == 0) & (pl.program_id(1) == 0))
    def _():
        def fold(base, sign):
            def body(e1, carry):
                wr = twr_ref[pl.ds(e1, 1), :]
                wi = twi_ref[pl.ds(e1, 1), :]
                cr = m3_ref[base] * wr - m3_ref[base + 1] * wi
                ci = m3_ref[base] * wi + m3_ref[base + 1] * wr
                for p, val in enumerate((cr, ci, cr + sign * ci)):
                    m3v_ref[e1, p] = val.astype(bf16)
                return carry
            return body
        lax.fori_loop(0, half1, fold(0, 1.0), 0)
        lax.fori_loop(half1, r1, fold(2, -1.0), 0)

    def stage1(d1, carry):
        rhs = jnp.concatenate(
            [v_ref[0, pl.ds(pl.multiple_of(k2 * d2 + CHUNK * d1, CHUNK), CHUNK), :] for d2 in range(16)], axis=0)
        r = jnp.dot(m1_ref[...], rhs, preferred_element_type=jnp.float32)
        ur_ref[0, pl.ds(pl.multiple_of(256 * d1, 256), 256), :] = r.astype(bf16)
        return carry

    lax.fori_loop(0, r1, stage1, 0, unroll=_unroll(r1, 64))

    def stage2(d0, carry):
        off = pl.multiple_of(CHUNK * d0, CHUNK)
        rhs = jnp.concatenate([ur_ref[0, pl.ds(256 * d1 + off, CHUNK), :] for d1 in range(r1)], axis=0)
        row = 0
        for n_e1 in [E1_GROUP] * (half1 // E1_GROUP) + [1]:
            rows = CHUNK * n_e1
            r = jnp.dot(m2_ref[2 * row:2 * (row + rows), :], rhs, preferred_element_type=jnp.float32)
            yr, yi = r[:rows], r[rows:]
            for p, val in enumerate((yr, yi, yr + yi)):
                y2_ref[d0, p, row:row + rows, :] = val.astype(bf16)
            row += rows
        return carry

    lax.fori_loop(0, 16, stage2, 0, unroll=_unroll(16, r1 * r1 // 4 + 64))

    def store(e1, re, im):
        off = pl.multiple_of(CHUNK * e1, CHUNK)
        for e2 in range(16):
            row = pl.multiple_of(k2 * e2 + off, CHUNK)
            ur_ref[0, pl.ds(row, CHUNK), :] = re[CHUNK * e2:CHUNK * (e2 + 1)].astype(bf16)
            ui_ref[0, pl.ds(row, CHUNK), :] = im[CHUNK * e2:CHUNK * (e2 + 1)].astype(bf16)

    def stage3_low(e1, carry):
        off = pl.multiple_of(CHUNK * e1, CHUNK)
        t1, t2, t3 = (
            jnp.dot(m3v_ref[e1, p],
                    jnp.concatenate([y2_ref[d0, p, pl.ds(off, CHUNK), :] for d0 in range(16)], axis=0),
                    preferred_element_type=jnp.float32)
            for p in range(3))
        store(e1, t1 - t2, t3 - t1 - t2)
        return carry

    lax.fori_loop(0, half1, stage3_low, 0, unroll=_unroll(half1, 192))

    first_row = lax.broadcasted_iota(jnp.int32, (CHUNK, 1), 0) == 0

    def stage3_high(e1, carry):
        off_a = pl.multiple_of(CHUNK * (r1 - 1 - e1), CHUNK)
        off_b = pl.multiple_of(CHUNK * (r1 - e1), CHUNK)
        t1, t2, t3 = (
            jnp.dot(m3v_ref[e1, p],
                    jnp.concatenate([jnp.where(first_row, y2_ref[d0, p, pl.ds(off_b, CHUNK), :],
                                               y2_ref[d0, p, pl.ds(off_a, CHUNK), :]) for d0 in range(16)], axis=0),
                    preferred_element_type=jnp.float32)
            for p in range(3))
        store(e1, t1 + t2, t1 - t2 - t3)
        return carry

    lax.fori_loop(half1, r1, stage3_high, 0, unroll=_unroll(half1, 192))


def _seq_dft(v, side=None):
    b, s, c = v.shape
    r1 = s // 256
    m1, m2, m3, twr, twi = _fft_constants(s)
    m1 = jnp.asarray(m1).astype(jnp.bfloat16)
    m2 = jnp.asarray(m2).astype(jnp.bfloat16)
    blk = pl.BlockSpec((1, s, FFT_LANES), lambda i, j: (i, 0, j))
    const2 = lambda i, j: (0, 0)
    nj = c // FFT_LANES
    in_specs = [
        blk,
        pl.BlockSpec(m1.shape, const2),
        pl.BlockSpec(m2.shape, const2),
        pl.BlockSpec(m3.shape, lambda i, j: (0, 0, 0)),
        pl.BlockSpec(twr.shape, const2),
        pl.BlockSpec(twi.shape, const2),
    ]
    operands = [v, m1, m2, jnp.asarray(m3), jnp.asarray(twr), jnp.asarray(twi)]
    out_specs = [blk, blk]
    out_shape = [jax.ShapeDtypeStruct((b, s, c), jnp.bfloat16)] * 2
    if side is not None:
        x, mod, row0, w_vf = side
        bs, ss, d = x.shape
        rows = bs * ss // (b * nj)
        assert rows * b * nj == bs * ss and ss % rows == 0 and rows % CHUNK == 0, (x.shape, v.shape)
        per_seq = ss // rows
        tile = lambda i, j: (i * nj + j, 0, 0)
        mod_spec = lambda k: pl.BlockSpec((1, 1, d), lambda i, j: (row0 + (i * nj + j) // per_seq, 0, k))
        in_specs += [pl.BlockSpec((1, rows, d), tile), mod_spec(MOD_SCALE), mod_spec(MOD_SHIFT),
                     pl.BlockSpec(w_vf.shape, const2)]
        operands += [x.reshape(bs * per_seq, rows, d), mod, mod, w_vf]
        out_specs.append(pl.BlockSpec((1, rows, D_F), tile))
        out_shape.append(jax.ShapeDtypeStruct((bs * per_seq, rows, D_F), jnp.bfloat16))
    outs = pl.pallas_call(
        functools.partial(_fft_kernel, r1=r1, side=side is not None),
        grid=(b, nj),
        in_specs=in_specs,
        out_specs=out_specs,
        out_shape=out_shape,
        scratch_shapes=[pltpu.VMEM((16, 3, CHUNK * (r1 // 2 + 1), FFT_LANES), jnp.bfloat16),
                        pltpu.VMEM((r1, 3, 256, 256), jnp.bfloat16)],
        compiler_params=pltpu.CompilerParams(
            dimension_semantics=("arbitrary", "arbitrary"), vmem_limit_bytes=VMEM_LIMIT),
        name="seq_dft",
    )(*operands)
    if side is not None:
        return outs[0], outs[1], outs[2].reshape(bs, ss, D_F)
    return outs


def _main_kernel(x_ref, xp_ref, xn_ref, ur_ref, ui_ref, scale_ref, shift_ref, gate_ref,
                 w_ref, g_ref, wo_ref, cw_ref, cb_ref, lg_ref, lb_ref, o_ref, *, tm):
    i = pl.program_id(1)
    n_tiles = pl.num_programs(1)
    bf16 = jnp.bfloat16
    x = x_ref[0]
    s1 = 1.0 + scale_ref[0]
    sh = shift_ref[0]
    h = x * s1 + sh
    h_halo = jnp.concatenate([xp_ref[0], xn_ref[0]], axis=0) * s1 + sh
    h_ext = jnp.concatenate([h, h_halo], axis=0).astype(bf16)

    hb = h_ext[:tm]
    ur = ur_ref[0]
    ui = ui_ref[0]
    rows = lax.broadcasted_iota(jnp.int32, (tm, 1), 0)
    y_a, y_f = [], []
    for c in range(D_A // COL_BLOCK):
        lo, hi = c * COL_BLOCK, (c + 1) * COL_BLOCK
        w0 = c * 5 * COL_BLOCK
        zc = jnp.dot(h_ext, w_ref[:, w0:w0 + 2 * COL_BLOCK], preferred_element_type=jnp.float32)
        u_all = zc[:, :COL_BLOCK] * zc[:, COL_BLOCK:]
        u = u_all[:tm]
        u_prev = jnp.where(i > 0, u_all[tm + 7:tm + 8], 0.0)
        u_next = jnp.where(i < n_tiles - 1, u_all[tm + 8:tm + 9], 0.0)
        u_m1 = jnp.where(rows == 0, u_prev, pltpu.roll(u, 1, 0))
        u_p1 = jnp.where(rows == tm - 1, u_next, pltpu.roll(u, tm - 1, 0))
        conv = (u_m1 * cw_ref[0:1, lo:hi] + u * cw_ref[1:2, lo:hi] + u_p1 * cw_ref[2:3, lo:hi]
                + cb_ref[:, lo:hi])

        zb = jnp.dot(hb, w_ref[:, w0 + 2 * COL_BLOCK:w0 + 5 * COL_BLOCK], preferred_element_type=jnp.float32)
        bg, ga, gf = (zb[:, k * COL_BLOCK:(k + 1) * COL_BLOCK] for k in range(3))
        y_a.append((bg * conv * jax.nn.silu(ga)).astype(bf16))

        lhs = jnp.concatenate([ur[:, lo:hi], ui[:, lo:hi]], axis=1)
        fr = jnp.dot(lhs, g_ref[c], preferred_element_type=jnp.float32)
        y_f.append((fr * jax.nn.silu(gf)).astype(bf16))

    y = jnp.concatenate(y_a + y_f, axis=1)
    for m in range(0, tm, OUT_ROWS):
        o = jnp.dot(y[m:m + OUT_ROWS], wo_ref[...], preferred_element_type=jnp.float32)
        r = ALPHA * x[m:m + OUT_ROWS] + gate_ref[0] * o
        mu = jnp.mean(r, axis=-1, keepdims=True)
        rc = r - mu
        var = jnp.mean(rc * rc, axis=-1, keepdims=True)
        o_ref[0, m:m + OUT_ROWS, :] = rc * lax.rsqrt(var + LN_EPS) * lg_ref[...] + lb_ref[...]


def _main(x, ur, ui, mod, row0, w_in, gfold, wo, conv_w, conv_b, ln_g, ln_b, tm):
    b, s, d = x.shape
    n8 = tm // 8
    last8 = s // 8 - 1
    tile = lambda i, j: (i, j, 0)
    const2 = lambda i, j: (0, 0)
    mod_spec = lambda k: pl.BlockSpec((1, 1, d), lambda i, j: (row0 + i, 0, k))
    return pl.pallas_call(
        functools.partial(_main_kernel, tm=tm),
        grid=(b, s // tm),
        in_specs=[
            pl.BlockSpec((1, tm, d), tile),
            pl.BlockSpec((1, 8, d), lambda i, j: (i, jnp.maximum(j * n8 - 1, 0), 0)),
            pl.BlockSpec((1, 8, d), lambda i, j: (i, jnp.minimum((j + 1) * n8, last8), 0)),
            pl.BlockSpec((1, tm, D_F), tile),
            pl.BlockSpec((1, tm, D_F), tile),
            mod_spec(MOD_SCALE),
            mod_spec(MOD_SHIFT),
            mod_spec(MOD_GATE),
            pl.BlockSpec(w_in.shape, const2),
            pl.BlockSpec(gfold.shape, lambda i, j: (0, 0, 0)),
            pl.BlockSpec(wo.shape, const2),
            pl.BlockSpec(conv_w.shape, const2),
            pl.BlockSpec(conv_b.shape, const2),
            pl.BlockSpec(ln_g.shape, const2),
            pl.BlockSpec(ln_b.shape, const2),
        ],
        out_specs=pl.BlockSpec((1, tm, d), tile),
        out_shape=jax.ShapeDtypeStruct((b, s, d), jnp.float32),
        compiler_params=pltpu.CompilerParams(
            dimension_semantics=("arbitrary", "arbitrary"), vmem_limit_bytes=VMEM_LIMIT),
        name="mixer_main",
    )(x, x, x, ur, ui, mod, mod, mod, w_in, gfold, wo, conv_w, conv_b, ln_g, ln_b)


def kernel(x_prompt, x_sample, c_prompt, c_sample, w_ada, b_ada, w_in, conv_w, conv_b,
           w_fmix, w_out, ln_g, ln_b):
    bf16 = jnp.bfloat16
    assert w_in.shape == (1, D_MODEL, 4 * D_A + 2 * D_F) and w_out.shape == (1, D_A + D_F, D_MODEL)
    for x in (x_prompt, x_sample):
        seq = x.shape[1]
        assert x.shape[2] == D_MODEL and seq // 256 in (16, 32) and seq % 256 == 0, x.shape
        assert seq % VF_ROWS == 0 and seq % MAIN_ROWS == 0, x.shape
    w_ada, b_ada, w_in, conv_w, conv_b = w_ada[0], b_ada[0], w_in[0], conv_w[0], conv_b[0]
    w_fmix, w_out, ln_g, ln_b = w_fmix[0], w_out[0], ln_g[0], ln_b[0]

    w_main, w_vf = _regroup_w_in(w_in)
    wo = w_out.astype(bf16)
    gfold = _fold_fmix(w_fmix)

    nb = c_prompt.shape[0]
    mod = _modulation(jnp.concatenate([c_prompt, c_sample], axis=0), w_ada, b_ada)
    mod = mod.reshape(mod.shape[0], 1, 3 * D_MODEL)

    main = functools.partial(_main, w_in=w_main, gfold=gfold, wo=wo, conv_w=conv_w, conv_b=conv_b.reshape(1, -1),
                             ln_g=ln_g.reshape(1, -1), ln_b=ln_b.reshape(1, -1), tm=MAIN_ROWS)
    vf_prompt = _vf_proj(x_prompt, mod, 0, w_vf, tm=VF_ROWS)
    ur, ui, vf_sample = _seq_dft(vf_prompt, side=(x_sample, mod, nb, w_vf))
    y_prompt = main(x_prompt, ur, ui, mod, 0)
    ur, ui = _seq_dft(vf_sample)
    y_sample = main(x_sample, ur, ui, mod, nb)
    return y_prompt, y_sample
```

```python
import functools

import numpy as np
import jax
import jax.numpy as jnp
from jax import lax
from jax.experimental import pallas as pl
from jax.experimental.pallas import tpu as pltpu

D_MODEL = 1024
HEAD_DIM = 128
N_GROUPS = 8
D_A = 1024
D_F = 1024
ALPHA = float(2.0 ** 0.25)
LN_EPS = 1e-5

COL_BG, COL_CG, COL_VA, COL_GA, COL_VF, COL_GF = range(6)
MOD_SHIFT, MOD_SCALE, MOD_GATE = range(3)
VF_ROWS = 2048
MAIN_ROWS = 1024
OUT_ROWS = 256
COL_BLOCK = 256

CHUNK = 16
FFT_LANES = 256
VMEM_LIMIT = 56 * 1024 * 1024
FFT_BLOCK_CYCLES = 4096
E1_GROUP = 8


def _e1_groups(r1):
    n = r1 // 2 // E1_GROUP
    return [E1_GROUP] * (n - 1) + [E1_GROUP + 1]


def _fft_constants(seq):
    r1 = seq // 256
    eye = np.eye(16)
    d = np.arange(16)
    pk = np.where(d[:, None] <= 8, np.cos(2 * np.pi * d[None, :] * d[:, None] / 16),
                  -np.sin(2 * np.pi * d[None, :] * (d[:, None] - 8) / 16))
    m1 = np.einsum('ab,pd->apdb', eye, pk).reshape(256, 256)

    unpack = np.zeros((16, 16), complex)
    unpack[0, 0] = unpack[8, 8] = 1
    for e0 in range(1, 8):
        unpack[e0, e0], unpack[e0, 8 + e0] = 1, 1j
        unpack[16 - e0, e0], unpack[16 - e0, 8 + e0] = 1, -1j
    dd = np.arange(r1)
    lo = np.arange(r1 // 2 + 1)
    ph = dd[None, None, :] * lo[:, None, None] / r1 + dd[None, None, :] * d[None, :, None] / (16.0 * r1)
    c = np.exp(-2j * np.pi * ph)
    m2c = np.einsum('aed,ep->aedp', c, unpack).reshape(16 * len(lo), 16 * r1)
    bounds = np.cumsum([0] + _e1_groups(r1)) * 16
    m2 = np.concatenate([np.concatenate([m2c.real[i:j], m2c.imag[i:j]], axis=0)
                         for i, j in zip(bounds[:-1], bounds[1:])], axis=0)

    ph = (d[None, None, :] * d[:, None, None]) / 16.0 + d[None, None, :] * d[None, :, None] / float(seq)
    c = np.exp(-2j * np.pi * ph) / np.sqrt(seq)
    mirror = np.zeros((16, 16))
    mirror[(16 - d) % 16, d] = 1
    cf = np.einsum('ald,lm->aldm', c, eye).reshape(256, 256)
    cfu = np.einsum('ald,lm->aldm', c, mirror).reshape(256, 256)
    m3 = np.stack([cf.real, cf.imag, cfu.real, cfu.imag])

    tw = np.repeat(np.exp(-2j * np.pi * np.outer(dd, d) / (16.0 * r1)), 16, axis=1)
    f32 = np.float32
    return m1.astype(f32), m2.astype(f32), m3.astype(f32), tw.real.astype(f32), tw.imag.astype(f32)


def _channel_dft():
    n = np.arange(HEAD_DIM)
    ang = 2 * np.pi * np.outer(n, n) / HEAD_DIM
    cs = np.concatenate([np.cos(ang), np.sin(ang)], axis=0) / np.sqrt(HEAD_DIM)
    return cs.astype(np.float32)


def _mod_kernel(c_ref, w_ref, b_ref, o_ref):
    c = c_ref[...]
    o_ref[...] = jnp.dot(jax.nn.silu(c), w_ref[...], precision=lax.Precision.HIGHEST,
                         preferred_element_type=jnp.float32) + b_ref[...]


def _modulation(c_all, w_ada, b_ada):
    n, d = c_all.shape
    tn = 512
    return pl.pallas_call(
        _mod_kernel,
        grid=(3 * D_MODEL // tn,),
        in_specs=[
            pl.BlockSpec((n, d), lambda j: (0, 0)),
            pl.BlockSpec((d, tn), lambda j: (0, j)),
            pl.BlockSpec((1, tn), lambda j: (0, j)),
        ],
        out_specs=pl.BlockSpec((n, tn), lambda j: (0, j)),
        out_shape=jax.ShapeDtypeStruct((n, 3 * D_MODEL), jnp.float32),
        compiler_params=pltpu.CompilerParams(dimension_semantics=("arbitrary",)),
        name="adaln_mod",
    )(c_all, w_ada, b_ada.reshape(1, -1))


def _fold_kernel(cs_ref, w_ref, o_ref):
    hd = HEAD_DIM
    o_ref[...] = jnp.zeros(o_ref.shape, o_ref.dtype)
    for g in range(N_GROUPS):
        t = jnp.dot(cs_ref[...], w_ref[g], precision=lax.Precision.HIGHEST,
                    preferred_element_type=jnp.float32).astype(o_ref.dtype)
        p, k = divmod(g, 2)
        o_ref[p, k * hd:(k + 1) * hd, k * hd:(k + 1) * hd] = t[:hd]
        o_ref[p, (2 + k) * hd:(3 + k) * hd, k * hd:(k + 1) * hd] = t[hd:]


def _fold_fmix(w_fmix):
    cs = jnp.asarray(_channel_dft())
    return pl.pallas_call(
        _fold_kernel,
        out_shape=jax.ShapeDtypeStruct((N_GROUPS // 2, 4 * HEAD_DIM, 2 * HEAD_DIM), jnp.bfloat16),
        name="fold_fmix",
    )(cs, w_fmix)


def _regroup_kernel(cg_ref, va_ref, bg_ref, ga_ref, gf_ref, vf_ref, main_ref, vfo_ref):
    for k, ref in enumerate((cg_ref, va_ref, bg_ref, ga_ref, gf_ref)):
        main_ref[:, k * COL_BLOCK:(k + 1) * COL_BLOCK] = ref[...].astype(main_ref.dtype)
    vfo_ref[...] = vf_ref[...].astype(vfo_ref.dtype)


def _regroup_w_in(w_in):
    d = w_in.shape[0]
    n_blocks = D_A // COL_BLOCK
    src = lambda k: pl.BlockSpec((d, COL_BLOCK), lambda c: (0, k * n_blocks + c))
    return pl.pallas_call(
        _regroup_kernel,
        grid=(n_blocks,),
        in_specs=[src(k) for k in (COL_CG, COL_VA, COL_BG, COL_GA, COL_GF, COL_VF)],
        out_specs=[pl.BlockSpec((d, 5 * COL_BLOCK), lambda c: (0, c)),
                   pl.BlockSpec((d, COL_BLOCK), lambda c: (0, c))],
        out_shape=[jax.ShapeDtypeStruct((d, 5 * D_A), jnp.bfloat16),
                   jax.ShapeDtypeStruct((d, D_F), jnp.bfloat16)],
        compiler_params=pltpu.CompilerParams(dimension_semantics=("arbitrary",)),
        name="regroup_w_in",
    )(w_in, w_in, w_in, w_in, w_in, w_in)


def _vf_kernel(x_ref, scale_ref, shift_ref, w_ref, o_ref):
    h = x_ref[0] * (1.0 + scale_ref[0]) + shift_ref[0]
    o_ref[0] = jnp.dot(h.astype(jnp.bfloat16), w_ref[...],
                       preferred_element_type=jnp.float32).astype(o_ref.dtype)


def _vf_proj(x, mod, row0, w_in, tm):
    b, s, d = x.shape
    return pl.pallas_call(
        _vf_kernel,
        grid=(b, s // tm),
        in_specs=[
            pl.BlockSpec((1, tm, d), lambda i, j: (i, j, 0)),
            pl.BlockSpec((1, 1, d), lambda i, j: (row0 + i, 0, MOD_SCALE)),
            pl.BlockSpec((1, 1, d), lambda i, j: (row0 + i, 0, MOD_SHIFT)),
            pl.BlockSpec((d, D_F), lambda i, j: (0, 0)),
        ],
        out_specs=pl.BlockSpec((1, tm, D_F), lambda i, j: (i, j, 0)),
        out_shape=jax.ShapeDtypeStruct((b, s, D_F), jnp.bfloat16),
        compiler_params=pltpu.CompilerParams(
            dimension_semantics=("arbitrary", "arbitrary"), vmem_limit_bytes=VMEM_LIMIT),
        name="vf_proj",
    )(x, mod, mod, w_in)


def _unroll(trips, mxu_cycles):
    want = max(1, min(trips, FFT_BLOCK_CYCLES // mxu_cycles))
    return max(u for u in range(1, want + 1) if trips % u == 0)


def _fft_kernel(v_ref, m1_ref, m2_ref, m3_ref, twr_ref, twi_ref, ur_ref, ui_ref, y2_ref, m3v_ref, *, r1):
    k2 = CHUNK * r1
    half1 = r1 // 2
    bf16 = jnp.bfloat16

    @pl.when((pl.program_id(0) == 0) & (pl.program_id(1) == 0))
    def _():
        def fold(base, sign):
            def body(e1, carry):
                wr = twr_ref[pl.ds(e1, 1), :]
                wi = twi_ref[pl.ds(e1, 1), :]
                cr = m3_ref[base] * wr - m3_ref[base + 1] * wi
                ci = m3_ref[base] * wi + m3_ref[base + 1] * wr
                for p, val in enumerate((cr, ci, cr + sign * ci)):
                    m3v_ref[e1, p] = val.astype(bf16)
                return carry
            return body
        lax.fori_loop(0, half1, fold(0, 1.0), 0)
        lax.fori_loop(half1, r1, fold(2, -1.0), 0)

    def stage1(d1, carry):
        rhs = jnp.concatenate(
            [v_ref[0, pl.ds(pl.multiple_of(k2 * d2 + CHUNK * d1, CHUNK), CHUNK), :] for d2 in range(16)], axis=0)
        r = jnp.dot(m1_ref[...], rhs, preferred_element_type=jnp.float32)
        ur_ref[0, pl.ds(pl.multiple_of(256 * d1, 256), 256), :] = r.astype(bf16)
        return carry

    lax.fori_loop(0, r1, stage1, 0, unroll=_unroll(r1, 64))

    def stage2(d0, carry):
        off = pl.multiple_of(CHUNK * d0, CHUNK)
        rhs = jnp.concatenate([ur_ref[0, pl.ds(256 * d1 + off, CHUNK), :] for d1 in range(r1)], axis=0)
        row = 0
        for n_e1 in _e1_groups(r1):
            rows = CHUNK * n_e1
            r = jnp.dot(m2_ref[2 * row:2 * (row + rows), :], rhs, preferred_element_type=jnp.float32)
            yr, yi = r[:rows], r[rows:]
            for p, val in enumerate((yr, yi, yr + yi)):
                y2_ref[d0, p, row:row + rows, :] = val.astype(bf16)
            row += rows
        return carry

    lax.fori_loop(0, 16, stage2, 0, unroll=_unroll(16, r1 * r1 // 4 + 64))

    def store(e1, re, im):
        off = pl.multiple_of(CHUNK * e1, CHUNK)
        for e2 in range(16):
            row = pl.multiple_of(k2 * e2 + off, CHUNK)
            ur_ref[0, pl.ds(row, CHUNK), :] = re[CHUNK * e2:CHUNK * (e2 + 1)].astype(bf16)
            ui_ref[0, pl.ds(row, CHUNK), :] = im[CHUNK * e2:CHUNK * (e2 + 1)].astype(bf16)

    def stage3_low(e1, carry):
        off = pl.multiple_of(CHUNK * e1, CHUNK)
        t1, t2, t3 = (
            jnp.dot(m3v_ref[e1, p],
                    jnp.concatenate([y2_ref[d0, p, pl.ds(off, CHUNK), :] for d0 in range(16)], axis=0),
                    preferred_element_type=jnp.float32)
            for p in range(3))
        store(e1, t1 - t2, t3 - t1 - t2)
        return carry

    lax.fori_loop(0, half1, stage3_low, 0, unroll=_unroll(half1, 192))

    first_row = lax.broadcasted_iota(jnp.int32, (CHUNK, 1), 0) == 0

    def stage3_high(e1, carry):
        off_a = pl.multiple_of(CHUNK * (r1 - 1 - e1), CHUNK)
        off_b = pl.multiple_of(CHUNK * (r1 - e1), CHUNK)
        t1, t2, t3 = (
            jnp.dot(m3v_ref[e1, p],
                    jnp.concatenate([jnp.where(first_row, y2_ref[d0, p, pl.ds(off_b, CHUNK), :],
                                               y2_ref[d0, p, pl.ds(off_a, CHUNK), :]) for d0 in range(16)], axis=0),
                    preferred_element_type=jnp.float32)
            for p in range(3))
        store(e1, t1 + t2, t1 - t2 - t3)
        return carry

    lax.fori_loop(half1, r1, stage3_high, 0, unroll=_unroll(half1, 192))


def _seq_dft(v):
    b, s, c = v.shape
    r1 = s // 256
    m1, m2, m3, twr, twi = _fft_constants(s)
    m1 = jnp.asarray(m1).astype(jnp.bfloat16)
    m2 = jnp.asarray(m2).astype(jnp.bfloat16)
    blk = pl.BlockSpec((1, s, FFT_LANES), lambda i, j: (i, 0, j))
    const2 = lambda i, j: (0, 0)
    return pl.pallas_call(
        functools.partial(_fft_kernel, r1=r1),
        grid=(b, c // FFT_LANES),
        in_specs=[
            blk,
            pl.BlockSpec(m1.shape, const2),
            pl.BlockSpec(m2.shape, const2),
            pl.BlockSpec(m3.shape, lambda i, j: (0, 0, 0)),
            pl.BlockSpec(twr.shape, const2),
            pl.BlockSpec(twi.shape, const2),
        ],
        out_specs=[blk, blk],
        out_shape=[jax.ShapeDtypeStruct((b, s, c), jnp.bfloat16)] * 2,
        scratch_shapes=[pltpu.VMEM((16, 3, CHUNK * (r1 // 2 + 1), FFT_LANES), jnp.bfloat16),
                        pltpu.VMEM((r1, 3, 256, 256), jnp.bfloat16)],
        compiler_params=pltpu.CompilerParams(
            dimension_semantics=("arbitrary", "arbitrary"), vmem_limit_bytes=VMEM_LIMIT),
        name="seq_dft",
    )(v, m1, m2, jnp.asarray(m3), jnp.asarray(twr), jnp.asarray(twi))


def _main_kernel(x_ref, xp_ref, xn_ref, ur_ref, ui_ref, scale_ref, shift_ref, gate_ref,
                 w_ref, g_ref, wo_ref, cw_ref, cb_ref, lg_ref, lb_ref, o_ref, *, tm):
    i = pl.program_id(1)
    n_tiles = pl.num_programs(1)
    bf16 = jnp.bfloat16
    x = x_ref[0]
    s1 = 1.0 + scale_ref[0]
    sh = shift_ref[0]
    h = x * s1 + sh
    h_halo = jnp.concatenate([xp_ref[0], xn_ref[0]], axis=0) * s1 + sh
    h_ext = jnp.concatenate([h, h_halo], axis=0).astype(bf16)

    hb = h_ext[:tm]
    ur = ur_ref[0]
    ui = ui_ref[0]
    rows = lax.broadcasted_iota(jnp.int32, (tm, 1), 0)
    y_a, y_f = [], []
    for c in range(D_A // COL_BLOCK):
        lo, hi = c * COL_BLOCK, (c + 1) * COL_BLOCK
        w0 = c * 5 * COL_BLOCK
        zc = jnp.dot(h_ext, w_ref[:, w0:w0 + 2 * COL_BLOCK], preferred_element_type=jnp.float32)
        u_all = zc[:, :COL_BLOCK] * zc[:, COL_BLOCK:]
        u = u_all[:tm]
        u_prev = jnp.where(i > 0, u_all[tm + 7:tm + 8], 0.0)
        u_next = jnp.where(i < n_tiles - 1, u_all[tm + 8:tm + 9], 0.0)
        u_m1 = jnp.where(rows == 0, u_prev, pltpu.roll(u, 1, 0))
        u_p1 = jnp.where(rows == tm - 1, u_next, pltpu.roll(u, tm - 1, 0))
        conv = (u_m1 * cw_ref[0:1, lo:hi] + u * cw_ref[1:2, lo:hi] + u_p1 * cw_ref[2:3, lo:hi]
                + cb_ref[:, lo:hi])

        zb = jnp.dot(hb, w_ref[:, w0 + 2 * COL_BLOCK:w0 + 5 * COL_BLOCK], preferred_element_type=jnp.float32)
        bg, ga, gf = (zb[:, k * COL_BLOCK:(k + 1) * COL_BLOCK] for k in range(3))
        y_a.append((bg * conv * jax.nn.silu(ga)).astype(bf16))

        lhs = jnp.concatenate([ur[:, lo:hi], ui[:, lo:hi]], axis=1)
        fr = jnp.dot(lhs, g_ref[c], preferred_element_type=jnp.float32)
        y_f.append((fr * jax.nn.silu(gf)).astype(bf16))

    y = jnp.concatenate(y_a + y_f, axis=1)
    for m in range(0, tm, OUT_ROWS):
        o = jnp.dot(y[m:m + OUT_ROWS], wo_ref[...], preferred_element_type=jnp.float32)
        r = ALPHA * x[m:m + OUT_ROWS] + gate_ref[0] * o
        mu = jnp.mean(r, axis=-1, keepdims=True)
        rc = r - mu
        var = jnp.mean(rc * rc, axis=-1, keepdims=True)
        o_ref[0, m:m + OUT_ROWS, :] = rc * lax.rsqrt(var + LN_EPS) * lg_ref[...] + lb_ref[...]


def _main(x, ur, ui, mod, row0, w_in, gfold, wo, conv_w, conv_b, ln_g, ln_b, tm):
    b, s, d = x.shape
    n8 = tm // 8
    last8 = s // 8 - 1
    tile = lambda i, j: (i, j, 0)
    const2 = lambda i, j: (0, 0)
    mod_spec = lambda k: pl.BlockSpec((1, 1, d), lambda i, j: (row0 + i, 0, k))
    return pl.pallas_call(
        functools.partial(_main_kernel, tm=tm),
        grid=(b, s // tm),
        in_specs=[
            pl.BlockSpec((1, tm, d), tile),
            pl.BlockSpec((1, 8, d), lambda i, j: (i, jnp.maximum(j * n8 - 1, 0), 0)),
            pl.BlockSpec((1, 8, d), lambda i, j: (i, jnp.minimum((j + 1) * n8, last8), 0)),
            pl.BlockSpec((1, tm, D_F), tile),
            pl.BlockSpec((1, tm, D_F), tile),
            mod_spec(MOD_SCALE),
            mod_spec(MOD_SHIFT),
            mod_spec(MOD_GATE),
            pl.BlockSpec(w_in.shape, const2),
            pl.BlockSpec(gfold.shape, lambda i, j: (0, 0, 0)),
            pl.BlockSpec(wo.shape, const2),
            pl.BlockSpec(conv_w.shape, const2),
            pl.BlockSpec(conv_b.shape, const2),
            pl.BlockSpec(ln_g.shape, const2),
            pl.BlockSpec(ln_b.shape, const2),
        ],
        out_specs=pl.BlockSpec((1, tm, d), tile),
        out_shape=jax.ShapeDtypeStruct((b, s, d), jnp.float32),
        compiler_params=pltpu.CompilerParams(
            dimension_semantics=("arbitrary", "arbitrary"), vmem_limit_bytes=VMEM_LIMIT),
        name="mixer_main",
    )(x, x, x, ur, ui, mod, mod, mod, w_in, gfold, wo, conv_w, conv_b, ln_g, ln_b)


def kernel(x_prompt, x_sample, c_prompt, c_sample, w_ada, b_ada, w_in, conv_w, conv_b,
           w_fmix, w_out, ln_g, ln_b):
    bf16 = jnp.bfloat16
    assert w_in.shape == (1, D_MODEL, 4 * D_A + 2 * D_F) and w_out.shape == (1, D_A + D_F, D_MODEL)
    for x in (x_prompt, x_sample):
        seq = x.shape[1]
        assert x.shape[2] == D_MODEL and seq // 256 in (16, 32) and seq % 256 == 0, x.shape
        assert seq % VF_ROWS == 0 and seq % MAIN_ROWS == 0, x.shape
    w_ada, b_ada, w_in, conv_w, conv_b = w_ada[0], b_ada[0], w_in[0], conv_w[0], conv_b[0]
    w_fmix, w_out, ln_g, ln_b = w_fmix[0], w_out[0], ln_g[0], ln_b[0]

    w_main, w_vf = _regroup_w_in(w_in)
    wo = w_out.astype(bf16)
    gfold = _fold_fmix(w_fmix)

    nb = c_prompt.shape[0]
    mod = _modulation(jnp.concatenate([c_prompt, c_sample], axis=0), w_ada, b_ada)
    mod = mod.reshape(mod.shape[0], 1, 3 * D_MODEL)

    outs = []
    for x, row0 in ((x_prompt, 0), (x_sample, nb)):
        vf = _vf_proj(x, mod, row0, w_vf, tm=VF_ROWS)
        ur, ui = _seq_dft(vf)
        outs.append(_main(x, ur, ui, mod, row0, w_main, gfold, wo, conv_w, conv_b.reshape(1, -1),
                          ln_g.reshape(1, -1), ln_b.reshape(1, -1), tm=MAIN_ROWS))
    return tuple(outs)
```

```python
import functools

import numpy as np
import jax
import jax.numpy as jnp
from jax import lax
from jax.experimental import pallas as pl
from jax.experimental.pallas import tpu as pltpu

D_MODEL = 1024
HEAD_DIM = 128
N_GROUPS = 8
D_A = 1024
D_F = 1024
ALPHA = float(2.0 ** 0.25)
LN_EPS = 1e-5

COL_BG, COL_CG, COL_VA, COL_GA, COL_VF, COL_GF = range(6)
MOD_SHIFT, MOD_SCALE, MOD_GATE = range(3)
VF_ROWS = 2048
MAIN_ROWS = 1024
OUT_ROWS = 256
COL_BLOCK = 256

CHUNK = 16
FFT_LANES = 256
VMEM_LIMIT = 56 * 1024 * 1024
FFT_BLOCK_CYCLES = 8192
E1_GROUP = 8


def _e1_groups(r1):
    n = r1 // 2 // E1_GROUP
    return [E1_GROUP] * (n - 1) + [E1_GROUP + 1]


def _fft_constants(seq):
    r1 = seq // 256
    eye = np.eye(16)
    d = np.arange(16)
    pk = np.where(d[:, None] <= 8, np.cos(2 * np.pi * d[None, :] * d[:, None] / 16),
                  -np.sin(2 * np.pi * d[None, :] * (d[:, None] - 8) / 16))
    m1 = np.einsum('ab,pd->apdb', eye, pk).reshape(256, 256)

    unpack = np.zeros((16, 16), complex)
    unpack[0, 0] = unpack[8, 8] = 1
    for e0 in range(1, 8):
        unpack[e0, e0], unpack[e0, 8 + e0] = 1, 1j
        unpack[16 - e0, e0], unpack[16 - e0, 8 + e0] = 1, -1j
    dd = np.arange(r1)
    lo = np.arange(r1 // 2 + 1)
    ph = dd[None, None, :] * lo[:, None, None] / r1 + dd[None, None, :] * d[None, :, None] / (16.0 * r1)
    c = np.exp(-2j * np.pi * ph)
    m2c = np.einsum('aed,ep->aedp', c, unpack).reshape(16 * len(lo), 16 * r1)
    bounds = np.cumsum([0] + _e1_groups(r1)) * 16
    m2 = np.concatenate([np.concatenate([m2c.real[i:j], m2c.imag[i:j]], axis=0)
                         for i, j in zip(bounds[:-1], bounds[1:])], axis=0)

    ph = (d[None, None, :] * d[:, None, None]) / 16.0 + d[None, None, :] * d[None, :, None] / float(seq)
    c = np.exp(-2j * np.pi * ph) / np.sqrt(seq)
    mirror = np.zeros((16, 16))
    mirror[(16 - d) % 16, d] = 1
    cf = np.einsum('ald,lm->aldm', c, eye).reshape(256, 256)
    cfu = np.einsum('ald,lm->aldm', c, mirror).reshape(256, 256)
    m3 = np.stack([cf.real, cf.imag, cfu.real, cfu.imag])

    tw = np.repeat(np.exp(-2j * np.pi * np.outer(dd, d) / (16.0 * r1)), 16, axis=1)
    f32 = np.float32
    return m1.astype(f32), m2.astype(f32), m3.astype(f32), tw.real.astype(f32), tw.imag.astype(f32)


def _channel_dft():
    n = np.arange(HEAD_DIM)
    ang = 2 * np.pi * np.outer(n, n) / HEAD_DIM
    cs = np.concatenate([np.cos(ang), np.sin(ang)], axis=0) / np.sqrt(HEAD_DIM)
    return cs.astype(np.float32)


def _mod_kernel(c_ref, w_ref, b_ref, o_ref):
    c = c_ref[...]
    o_ref[...] = jnp.dot(jax.nn.silu(c), w_ref[...], precision=lax.Precision.HIGHEST,
                         preferred_element_type=jnp.float32) + b_ref[...]


def _modulation(c_all, w_ada, b_ada):
    n, d = c_all.shape
    tn = 512
    return pl.pallas_call(
        _mod_kernel,
        grid=(3 * D_MODEL // tn,),
        in_specs=[
            pl.BlockSpec((n, d), lambda j: (0, 0)),
            pl.BlockSpec((d, tn), lambda j: (0, j)),
            pl.BlockSpec((1, tn), lambda j: (0, j)),
        ],
        out_specs=pl.BlockSpec((n, tn), lambda j: (0, j)),
        out_shape=jax.ShapeDtypeStruct((n, 3 * D_MODEL), jnp.float32),
        compiler_params=pltpu.CompilerParams(dimension_semantics=("arbitrary",)),
        name="adaln_mod",
    )(c_all, w_ada, b_ada.reshape(1, -1))


def _fold_kernel(cs_ref, w_ref, o_ref):
    hd = HEAD_DIM
    o_ref[...] = jnp.zeros(o_ref.shape, o_ref.dtype)
    for g in range(N_GROUPS):
        t = jnp.dot(cs_ref[...], w_ref[g], precision=lax.Precision.HIGHEST,
                    preferred_element_type=jnp.float32).astype(o_ref.dtype)
        p, k = divmod(g, 2)
        o_ref[p, k * hd:(k + 1) * hd, k * hd:(k + 1) * hd] = t[:hd]
        o_ref[p, (2 + k) * hd:(3 + k) * hd, k * hd:(k + 1) * hd] = t[hd:]


def _fold_fmix(w_fmix):
    cs = jnp.asarray(_channel_dft())
    return pl.pallas_call(
        _fold_kernel,
        out_shape=jax.ShapeDtypeStruct((N_GROUPS // 2, 4 * HEAD_DIM, 2 * HEAD_DIM), jnp.bfloat16),
        name="fold_fmix",
    )(cs, w_fmix)


def _regroup_kernel(cg_ref, va_ref, bg_ref, ga_ref, gf_ref, vf_ref, main_ref, vfo_ref):
    for k, ref in enumerate((cg_ref, va_ref, bg_ref, ga_ref, gf_ref)):
        main_ref[:, k * COL_BLOCK:(k + 1) * COL_BLOCK] = ref[...].astype(main_ref.dtype)
    vfo_ref[...] = vf_ref[...].astype(vfo_ref.dtype)


def _regroup_w_in(w_in):
    d = w_in.shape[0]
    n_blocks = D_A // COL_BLOCK
    src = lambda k: pl.BlockSpec((d, COL_BLOCK), lambda c: (0, k * n_blocks + c))
    return pl.pallas_call(
        _regroup_kernel,
        grid=(n_blocks,),
        in_specs=[src(k) for k in (COL_CG, COL_VA, COL_BG, COL_GA, COL_GF, COL_VF)],
        out_specs=[pl.BlockSpec((d, 5 * COL_BLOCK), lambda c: (0, c)),
                   pl.BlockSpec((d, COL_BLOCK), lambda c: (0, c))],
        out_shape=[jax.ShapeDtypeStruct((d, 5 * D_A), jnp.bfloat16),
                   jax.ShapeDtypeStruct((d, D_F), jnp.bfloat16)],
        compiler_params=pltpu.CompilerParams(dimension_semantics=("arbitrary",)),
        name="regroup_w_in",
    )(w_in, w_in, w_in, w_in, w_in, w_in)


def _vf_kernel(x_ref, scale_ref, shift_ref, w_ref, o_ref):
    h = x_ref[0] * (1.0 + scale_ref[0]) + shift_ref[0]
    o_ref[0] = jnp.dot(h.astype(jnp.bfloat16), w_ref[...],
                       preferred_element_type=jnp.float32).astype(o_ref.dtype)


def _vf_proj(x, mod, row0, w_in, tm):
    b, s, d = x.shape
    return pl.pallas_call(
        _vf_kernel,
        grid=(b, s // tm),
        in_specs=[
            pl.BlockSpec((1, tm, d), lambda i, j: (i, j, 0)),
            pl.BlockSpec((1, 1, d), lambda i, j: (row0 + i, 0, MOD_SCALE)),
            pl.BlockSpec((1, 1, d), lambda i, j: (row0 + i, 0, MOD_SHIFT)),
            pl.BlockSpec((d, D_F), lambda i, j: (0, 0)),
        ],
        out_specs=pl.BlockSpec((1, tm, D_F), lambda i, j: (i, j, 0)),
        out_shape=jax.ShapeDtypeStruct((b, s, D_F), jnp.bfloat16),
        compiler_params=pltpu.CompilerParams(
            dimension_semantics=("arbitrary", "arbitrary"), vmem_limit_bytes=VMEM_LIMIT),
        name="vf_proj",
    )(x, mod, mod, w_in)


def _unroll(trips, mxu_cycles):
    want = max(1, min(trips, FFT_BLOCK_CYCLES // mxu_cycles))
    return max(u for u in range(1, want + 1) if trips % u == 0)


def _fft_kernel(v_ref, m1_ref, m2_ref, m3_ref, twr_ref, twi_ref, ur_ref, ui_ref, y2_ref, m3v_ref, *, r1):
    k2 = CHUNK * r1
    half1 = r1 // 2
    bf16 = jnp.bfloat16

    @pl.when((pl.program_id(0) == 0) & (pl.program_id(1) == 0))
    def _():
        def fold(base, sign):
            def body(e1, carry):
                wr = twr_ref[pl.ds(e1, 1), :]
                wi = twi_ref[pl.ds(e1, 1), :]
                cr = m3_ref[base] * wr - m3_ref[base + 1] * wi
                ci = m3_ref[base] * wi + m3_ref[base + 1] * wr
                for p, val in enumerate((cr, ci, cr + sign * ci)):
                    m3v_ref[e1, p] = val.astype(bf16)
                return carry
            return body
        lax.fori_loop(0, half1, fold(0, 1.0), 0)
        lax.fori_loop(half1, r1, fold(2, -1.0), 0)

    def stage1(d1, carry):
        rhs = jnp.concatenate(
            [v_ref[0, pl.ds(pl.multiple_of(k2 * d2 + CHUNK * d1, CHUNK), CHUNK), :] for d2 in range(16)], axis=0)
        r = jnp.dot(m1_ref[...], rhs, preferred_element_type=jnp.float32)
        ur_ref[0, pl.ds(pl.multiple_of(256 * d1, 256), 256), :] = r.astype(bf16)
        return carry

    lax.fori_loop(0, r1, stage1, 0, unroll=_unroll(r1, 64))

    def stage2(d0, carry):
        off = pl.multiple_of(CHUNK * d0, CHUNK)
        rhs = jnp.concatenate([ur_ref[0, pl.ds(256 * d1 + off, CHUNK), :] for d1 in range(r1)], axis=0)
        row = 0
        for n_e1 in _e1_groups(r1):
            rows = CHUNK * n_e1
            r = jnp.dot(m2_ref[2 * row:2 * (row + rows), :], rhs, preferred_element_type=jnp.float32)
            yr, yi = r[:rows], r[rows:]
            for p, val in enumerate((yr, yi, yr + yi)):
                y2_ref[d0, p, row:row + rows, :] = val.astype(bf16)
            row += rows
        return carry

    lax.fori_loop(0, 16, stage2, 0, unroll=_unroll(16, r1 * r1 // 4 + 64))

    def store(e1, re, im):
        off = pl.multiple_of(CHUNK * e1, CHUNK)
        for e2 in range(16):
            row = pl.multiple_of(k2 * e2 + off, CHUNK)
            ur_ref[0, pl.ds(row, CHUNK), :] = re[CHUNK * e2:CHUNK * (e2 + 1)].astype(bf16)
            ui_ref[0, pl.ds(row, CHUNK), :] = im[CHUNK * e2:CHUNK * (e2 + 1)].astype(bf16)

    def stage3_low(e1, carry):
        off = pl.multiple_of(CHUNK * e1, CHUNK)
        t1, t2, t3 = (
            jnp.dot(m3v_ref[e1, p],
                    jnp.concatenate([y2_ref[d0, p, pl.ds(off, CHUNK), :] for d0 in range(16)], axis=0),
                    preferred_element_type=jnp.float32)
            for p in range(3))
        store(e1, t1 - t2, t3 - t1 - t2)
        return carry

    lax.fori_loop(0, half1, stage3_low, 0, unroll=_unroll(half1, 192))

    first_row = lax.broadcasted_iota(jnp.int32, (CHUNK, 1), 0) == 0

    def stage3_high(e1, carry):
        off_a = pl.multiple_of(CHUNK * (r1 - 1 - e1), CHUNK)
        off_b = pl.multiple_of(CHUNK * (r1 - e1), CHUNK)
        t1, t2, t3 = (
            jnp.dot(m3v_ref[e1, p],
                    jnp.concatenate([jnp.where(first_row, y2_ref[d0, p, pl.ds(off_b, CHUNK), :],
                                               y2_ref[d0, p, pl.ds(off_a, CHUNK), :]) for d0 in range(16)], axis=0),
                    preferred_element_type=jnp.float32)
            for p in range(3))
        store(e1, t1 + t2, t1 - t2 - t3)
        return carry

    lax.fori_loop(half1, r1, stage3_high, 0, unroll=_unroll(half1, 192))


def _seq_dft(v):
    b, s, c = v.shape
    r1 = s // 256
    m1, m2, m3, twr, twi = _fft_constants(s)
    m1 = jnp.asarray(m1).astype(jnp.bfloat16)
    m2 = jnp.asarray(m2).astype(jnp.bfloat16)
    blk = pl.BlockSpec((1, s, FFT_LANES), lambda i, j: (i, 0, j))
    const2 = lambda i, j: (0, 0)
    return pl.pallas_call(
        functools.partial(_fft_kernel, r1=r1),
        grid=(b, c // FFT_LANES),
        in_specs=[
            blk,
            pl.BlockSpec(m1.shape, const2),
            pl.BlockSpec(m2.shape, const2),
            pl.BlockSpec(m3.shape, lambda i, j: (0, 0, 0)),
            pl.BlockSpec(twr.shape, const2),
            pl.BlockSpec(twi.shape, const2),
        ],
        out_specs=[blk, blk],
        out_shape=[jax.ShapeDtypeStruct((b, s, c), jnp.bfloat16)] * 2,
        scratch_shapes=[pltpu.VMEM((16, 3, CHUNK * (r1 // 2 + 1), FFT_LANES), jnp.bfloat16),
                        pltpu.VMEM((r1, 3, 256, 256), jnp.bfloat16)],
        compiler_params=pltpu.CompilerParams(
            dimension_semantics=("arbitrary", "arbitrary"), vmem_limit_bytes=VMEM_LIMIT),
        name="seq_dft",
    )(v, m1, m2, jnp.asarray(m3), jnp.asarray(twr), jnp.asarray(twi))


def _main_kernel(x_ref, xp_ref, xn_ref, ur_ref, ui_ref, scale_ref, shift_ref, gate_ref,
                 w_ref, g_ref, wo_ref, cw_ref, cb_ref, lg_ref, lb_ref, o_ref, *, tm):
    i = pl.program_id(1)
    n_tiles = pl.num_programs(1)
    bf16 = jnp.bfloat16
    x = x_ref[0]
    s1 = 1.0 + scale_ref[0]
    sh = shift_ref[0]
    h = x * s1 + sh
    h_halo = jnp.concatenate([xp_ref[0], xn_ref[0]], axis=0) * s1 + sh
    h_ext = jnp.concatenate([h, h_halo], axis=0).astype(bf16)

    hb = h_ext[:tm]
    ur = ur_ref[0]
    ui = ui_ref[0]
    rows = lax.broadcasted_iota(jnp.int32, (tm, 1), 0)
    y_a, y_f = [], []
    for c in range(D_A // COL_BLOCK):
        lo, hi = c * COL_BLOCK, (c + 1) * COL_BLOCK
        w0 = c * 5 * COL_BLOCK
        zc = jnp.dot(h_ext, w_ref[:, w0:w0 + 2 * COL_BLOCK], preferred_element_type=jnp.float32)
        u_all = zc[:, :COL_BLOCK] * zc[:, COL_BLOCK:]
        u = u_all[:tm]
        u_prev = jnp.where(i > 0, u_all[tm + 7:tm + 8], 0.0)
        u_next = jnp.where(i < n_tiles - 1, u_all[tm + 8:tm + 9], 0.0)
        u_m1 = jnp.where(rows == 0, u_prev, pltpu.roll(u, 1, 0))
        u_p1 = jnp.where(rows == tm - 1, u_next, pltpu.roll(u, tm - 1, 0))
        conv = (u_m1 * cw_ref[0:1, lo:hi] + u * cw_ref[1:2, lo:hi] + u_p1 * cw_ref[2:3, lo:hi]
                + cb_ref[:, lo:hi])

        zb = jnp.dot(hb, w_ref[:, w0 + 2 * COL_BLOCK:w0 + 5 * COL_BLOCK], preferred_element_type=jnp.float32)
        bg, ga, gf = (zb[:, k * COL_BLOCK:(k + 1) * COL_BLOCK] for k in range(3))
        y_a.append((bg * conv * jax.nn.silu(ga)).astype(bf16))

        lhs = jnp.concatenate([ur[:, lo:hi], ui[:, lo:hi]], axis=1)
        fr = jnp.dot(lhs, g_ref[c], preferred_element_type=jnp.float32)
        y_f.append((fr * jax.nn.silu(gf)).astype(bf16))

    y = jnp.concatenate(y_a + y_f, axis=1)
    for m in range(0, tm, OUT_ROWS):
        o = jnp.dot(y[m:m + OUT_ROWS], wo_ref[...], preferred_element_type=jnp.float32)
        r = ALPHA * x[m:m + OUT_ROWS] + gate_ref[0] * o
        mu = jnp.mean(r, axis=-1, keepdims=True)
        rc = r - mu
        var = jnp.mean(rc * rc, axis=-1, keepdims=True)
        o_ref[0, m:m + OUT_ROWS, :] = rc * lax.rsqrt(var + LN_EPS) * lg_ref[...] + lb_ref[...]


def _main(x, ur, ui, mod, row0, w_in, gfold, wo, conv_w, conv_b, ln_g, ln_b, tm):
    b, s, d = x.shape
    n8 = tm // 8
    last8 = s // 8 - 1
    tile = lambda i, j: (i, j, 0)
    const2 = lambda i, j: (0, 0)
    mod_spec = lambda k: pl.BlockSpec((1, 1, d), lambda i, j: (row0 + i, 0, k))
    return pl.pallas_call(
        functools.partial(_main_kernel, tm=tm),
        grid=(b, s // tm),
        in_specs=[
            pl.BlockSpec((1, tm, d), tile),
            pl.BlockSpec((1, 8, d), lambda i, j: (i, jnp.maximum(j * n8 - 1, 0), 0)),
            pl.BlockSpec((1, 8, d), lambda i, j: (i, jnp.minimum((j + 1) * n8, last8), 0)),
            pl.BlockSpec((1, tm, D_F), tile),
            pl.BlockSpec((1, tm, D_F), tile),
            mod_spec(MOD_SCALE),
            mod_spec(MOD_SHIFT),
            mod_spec(MOD_GATE),
            pl.BlockSpec(w_in.shape, const2),
            pl.BlockSpec(gfold.shape, lambda i, j: (0, 0, 0)),
            pl.BlockSpec(wo.shape, const2),
            pl.BlockSpec(conv_w.shape, const2),
            pl.BlockSpec(conv_b.shape, const2),
            pl.BlockSpec(ln_g.shape, const2),
            pl.BlockSpec(ln_b.shape, const2),
        ],
        out_specs=pl.BlockSpec((1, tm, d), tile),
        out_shape=jax.ShapeDtypeStruct((b, s, d), jnp.float32),
        compiler_params=pltpu.CompilerParams(
            dimension_semantics=("arbitrary", "arbitrary"), vmem_limit_bytes=VMEM_LIMIT),
        name="mixer_main",
    )(x, x, x, ur, ui, mod, mod, mod, w_in, gfold, wo, conv_w, conv_b, ln_g, ln_b)


def kernel(x_prompt, x_sample, c_prompt, c_sample, w_ada, b_ada, w_in, conv_w, conv_b,
           w_fmix, w_out, ln_g, ln_b):
    bf16 = jnp.bfloat16
    assert w_in.shape == (1, D_MODEL, 4 * D_A + 2 * D_F) and w_out.shape == (1, D_A + D_F, D_MODEL)
    for x in (x_prompt, x_sample):
        seq = x.shape[1]
        assert x.shape[2] == D_MODEL and seq // 256 in (16, 32) and seq % 256 == 0, x.shape
        assert seq % VF_ROWS == 0 and seq % MAIN_ROWS == 0, x.shape
    w_ada, b_ada, w_in, conv_w, conv_b = w_ada[0], b_ada[0], w_in[0], conv_w[0], conv_b[0]
    w_fmix, w_out, ln_g, ln_b = w_fmix[0], w_out[0], ln_g[0], ln_b[0]

    w_main, w_vf = _regroup_w_in(w_in)
    wo = w_out.astype(bf16)
    gfold = _fold_fmix(w_fmix)

    nb = c_prompt.shape[0]
    mod = _modulation(jnp.concatenate([c_prompt, c_sample], axis=0), w_ada, b_ada)
    mod = mod.reshape(mod.shape[0], 1, 3 * D_MODEL)

    outs = []
    for x, row0 in ((x_prompt, 0), (x_sample, nb)):
        vf = _vf_proj(x, mod, row0, w_vf, tm=VF_ROWS)
        ur, ui = _seq_dft(vf)
        outs.append(_main(x, ur, ui, mod, row0, w_main, gfold, wo, conv_w, conv_b.reshape(1, -1),
                          ln_g.reshape(1, -1), ln_b.reshape(1, -1), tm=MAIN_ROWS))
    return tuple(outs)
```

```python
import functools

import numpy as np
import jax
import jax.numpy as jnp
from jax import lax
from jax.experimental import pallas as pl
from jax.experimental.pallas import tpu as pltpu

D_MODEL = 1024
HEAD_DIM = 128
N_GROUPS = 8
D_A = 1024
D_F = 1024
ALPHA = float(2.0 ** 0.25)
LN_EPS = 1e-5

COL_BG, COL_CG, COL_VA, COL_GA, COL_VF, COL_GF = range(6)
MOD_SHIFT, MOD_SCALE, MOD_GATE = range(3)
VF_ROWS = 2048
MAIN_ROWS = 1024
OUT_ROWS = 256
COL_BLOCK = 256

CHUNK = 16
FFT_LANES = 256
VMEM_LIMIT = 56 * 1024 * 1024
FFT_BLOCK_CYCLES = 8192
E1_GROUP = 8


def _e1_groups(r1):
    n = r1 // 2 // E1_GROUP
    return [E1_GROUP] * (n - 1) + [E1_GROUP + 1]


def _fft_constants(seq):
    r1 = seq // 256
    eye = np.eye(16)
    d = np.arange(16)
    pk = np.where(d[:, None] <= 8, np.cos(2 * np.pi * d[None, :] * d[:, None] / 16),
                  -np.sin(2 * np.pi * d[None, :] * (d[:, None] - 8) / 16))
    m1 = np.einsum('ab,pd->apdb', eye, pk).reshape(256, 256)

    unpack = np.zeros((16, 16), complex)
    unpack[0, 0] = unpack[8, 8] = 1
    for e0 in range(1, 8):
        unpack[e0, e0], unpack[e0, 8 + e0] = 1, 1j
        unpack[16 - e0, e0], unpack[16 - e0, 8 + e0] = 1, -1j
    dd = np.arange(r1)
    lo = np.arange(r1 // 2 + 1)
    ph = dd[None, None, :] * lo[:, None, None] / r1 + dd[None, None, :] * d[None, :, None] / (16.0 * r1)
    c = np.exp(-2j * np.pi * ph)
    m2c = np.einsum('aed,ep->aedp', c, unpack).reshape(16 * len(lo), 16 * r1)
    bounds = np.cumsum([0] + _e1_groups(r1)) * 16
    m2 = np.concatenate([np.concatenate([m2c.real[i:j], m2c.imag[i:j]], axis=0)
                         for i, j in zip(bounds[:-1], bounds[1:])], axis=0)

    ph = (d[None, None, :] * d[:, None, None]) / 16.0 + d[None, None, :] * d[None, :, None] / float(seq)
    c = np.exp(-2j * np.pi * ph) / np.sqrt(seq)
    mirror = np.zeros((16, 16))
    mirror[(16 - d) % 16, d] = 1
    cf = np.einsum('ald,lm->aldm', c, eye).reshape(256, 256)
    cfu = np.einsum('ald,lm->aldm', c, mirror).reshape(256, 256)
    m3 = np.stack([cf.real, cf.imag, cfu.real, cfu.imag])

    tw = np.repeat(np.exp(-2j * np.pi * np.outer(dd, d) / (16.0 * r1)), 16, axis=1)
    f32 = np.float32
    return m1.astype(f32), m2.astype(f32), m3.astype(f32), tw.real.astype(f32), tw.imag.astype(f32)


def _channel_dft():
    n = np.arange(HEAD_DIM)
    ang = 2 * np.pi * np.outer(n, n) / HEAD_DIM
    cs = np.concatenate([np.cos(ang), np.sin(ang)], axis=0) / np.sqrt(HEAD_DIM)
    return cs.astype(np.float32)


def _mod_kernel(c_ref, w_ref, b_ref, o_ref):
    bf16, f32 = jnp.bfloat16, jnp.float32
    n = c_ref.shape[0]
    s = jax.nn.silu(c_ref[...])
    w = w_ref[...]
    s_hi = s.astype(bf16)
    w_hi = w.astype(bf16)
    lhs = jnp.concatenate([s_hi, (s - s_hi.astype(f32)).astype(bf16)], axis=0)
    by_hi = jnp.dot(lhs, w_hi, preferred_element_type=f32)
    by_lo = jnp.dot(lhs, (w - w_hi.astype(f32)).astype(bf16), preferred_element_type=f32)
    o_ref[...] = by_hi[:n] + by_hi[n:] + by_lo[:n] + b_ref[...]


def _modulation(c_all, w_ada, b_ada):
    n, d = c_all.shape
    tn = 512
    return pl.pallas_call(
        _mod_kernel,
        grid=(3 * D_MODEL // tn,),
        in_specs=[
            pl.BlockSpec((n, d), lambda j: (0, 0)),
            pl.BlockSpec((d, tn), lambda j: (0, j)),
            pl.BlockSpec((1, tn), lambda j: (0, j)),
        ],
        out_specs=pl.BlockSpec((n, tn), lambda j: (0, j)),
        out_shape=jax.ShapeDtypeStruct((n, 3 * D_MODEL), jnp.float32),
        compiler_params=pltpu.CompilerParams(dimension_semantics=("arbitrary",)),
        name="adaln_mod",
    )(c_all, w_ada, b_ada.reshape(1, -1))


def _fold_kernel(cs_ref, w_ref, o_ref):
    hd = HEAD_DIM
    o_ref[...] = jnp.zeros(o_ref.shape, o_ref.dtype)
    for g in range(N_GROUPS):
        t = jnp.dot(cs_ref[...], w_ref[g], precision=lax.Precision.HIGHEST,
                    preferred_element_type=jnp.float32).astype(o_ref.dtype)
        p, k = divmod(g, 2)
        o_ref[p, k * hd:(k + 1) * hd, k * hd:(k + 1) * hd] = t[:hd]
        o_ref[p, (2 + k) * hd:(3 + k) * hd, k * hd:(k + 1) * hd] = t[hd:]


def _fold_fmix(w_fmix):
    cs = jnp.asarray(_channel_dft())
    return pl.pallas_call(
        _fold_kernel,
        out_shape=jax.ShapeDtypeStruct((N_GROUPS // 2, 4 * HEAD_DIM, 2 * HEAD_DIM), jnp.bfloat16),
        name="fold_fmix",
    )(cs, w_fmix)


def _regroup_kernel(cg_ref, va_ref, bg_ref, ga_ref, gf_ref, vf_ref, main_ref, vfo_ref):
    for k, ref in enumerate((cg_ref, va_ref, bg_ref, ga_ref, gf_ref)):
        main_ref[:, k * COL_BLOCK:(k + 1) * COL_BLOCK] = ref[...].astype(main_ref.dtype)
    vfo_ref[...] = vf_ref[...].astype(vfo_ref.dtype)


def _regroup_w_in(w_in):
    d = w_in.shape[0]
    n_blocks = D_A // COL_BLOCK
    src = lambda k: pl.BlockSpec((d, COL_BLOCK), lambda c: (0, k * n_blocks + c))
    return pl.pallas_call(
        _regroup_kernel,
        grid=(n_blocks,),
        in_specs=[src(k) for k in (COL_CG, COL_VA, COL_BG, COL_GA, COL_GF, COL_VF)],
        out_specs=[pl.BlockSpec((d, 5 * COL_BLOCK), lambda c: (0, c)),
                   pl.BlockSpec((d, COL_BLOCK), lambda c: (0, c))],
        out_shape=[jax.ShapeDtypeStruct((d, 5 * D_A), jnp.bfloat16),
                   jax.ShapeDtypeStruct((d, D_F), jnp.bfloat16)],
        compiler_params=pltpu.CompilerParams(dimension_semantics=("arbitrary",)),
        name="regroup_w_in",
    )(w_in, w_in, w_in, w_in, w_in, w_in)


def _vf_kernel(x_ref, scale_ref, shift_ref, w_ref, o_ref):
    h = x_ref[0] * (1.0 + scale_ref[0]) + shift_ref[0]
    o_ref[0] = jnp.dot(h.astype(jnp.bfloat16), w_ref[...],
                       preferred_element_type=jnp.float32).astype(o_ref.dtype)


def _vf_proj(x, mod, row0, w_in, tm):
    b, s, d = x.shape
    return pl.pallas_call(
        _vf_kernel,
        grid=(b, s // tm),
        in_specs=[
            pl.BlockSpec((1, tm, d), lambda i, j: (i, j, 0)),
            pl.BlockSpec((1, 1, d), lambda i, j: (row0 + i, 0, MOD_SCALE)),
            pl.BlockSpec((1, 1, d), lambda i, j: (row0 + i, 0, MOD_SHIFT)),
            pl.BlockSpec((d, D_F), lambda i, j: (0, 0)),
        ],
        out_specs=pl.BlockSpec((1, tm, D_F), lambda i, j: (i, j, 0)),
        out_shape=jax.ShapeDtypeStruct((b, s, D_F), jnp.bfloat16),
        compiler_params=pltpu.CompilerParams(
            dimension_semantics=("arbitrary", "arbitrary"), vmem_limit_bytes=VMEM_LIMIT),
        name="vf_proj",
    )(x, mod, mod, w_in)


def _unroll(trips, mxu_cycles):
    want = max(1, min(trips, FFT_BLOCK_CYCLES // mxu_cycles))
    return max(u for u in range(1, want + 1) if trips % u == 0)


def _fft_kernel(v_ref, m1_ref, m2_ref, m3_ref, twr_ref, twi_ref, ur_ref, ui_ref, y2_ref, m3v_ref, *, r1):
    k2 = CHUNK * r1
    half1 = r1 // 2
    bf16 = jnp.bfloat16

    @pl.when((pl.program_id(0) == 0) & (pl.program_id(1) == 0))
    def _():
        def fold(base, sign):
            def body(e1, carry):
                wr = twr_ref[pl.ds(e1, 1), :]
                wi = twi_ref[pl.ds(e1, 1), :]
                cr = m3_ref[base] * wr - m3_ref[base + 1] * wi
                ci = m3_ref[base] * wi + m3_ref[base + 1] * wr
                for p, val in enumerate((cr, ci, cr + sign * ci)):
                    m3v_ref[e1, p] = val.astype(bf16)
                return carry
            return body
        lax.fori_loop(0, half1, fold(0, 1.0), 0)
        lax.fori_loop(half1, r1, fold(2, -1.0), 0)

    def stage1(d1, carry):
        rhs = jnp.concatenate(
            [v_ref[0, pl.ds(pl.multiple_of(k2 * d2 + CHUNK * d1, CHUNK), CHUNK), :] for d2 in range(16)], axis=0)
        r = jnp.dot(m1_ref[...], rhs, preferred_element_type=jnp.float32)
        ur_ref[0, pl.ds(pl.multiple_of(256 * d1, 256), 256), :] = r.astype(bf16)
        return carry

    lax.fori_loop(0, r1, stage1, 0, unroll=_unroll(r1, 64))

    def stage2(d0, carry):
        off = pl.multiple_of(CHUNK * d0, CHUNK)
        rhs = jnp.concatenate([ur_ref[0, pl.ds(256 * d1 + off, CHUNK), :] for d1 in range(r1)], axis=0)
        row = 0
        for n_e1 in _e1_groups(r1):
            rows = CHUNK * n_e1
            r = jnp.dot(m2_ref[2 * row:2 * (row + rows), :], rhs, preferred_element_type=jnp.float32)
            yr, yi = r[:rows], r[rows:]
            for p, val in enumerate((yr, yi, yr + yi)):
                y2_ref[d0, p, row:row + rows, :] = val.astype(bf16)
            row += rows
        return carry

    lax.fori_loop(0, 16, stage2, 0, unroll=_unroll(16, r1 * r1 // 4 + 64))

    def store(e1, re, im):
        off = pl.multiple_of(CHUNK * e1, CHUNK)
        for e2 in range(16):
            row = pl.multiple_of(k2 * e2 + off, CHUNK)
            ur_ref[0, pl.ds(row, CHUNK), :] = re[CHUNK * e2:CHUNK * (e2 + 1)].astype(bf16)
            ui_ref[0, pl.ds(row, CHUNK), :] = im[CHUNK * e2:CHUNK * (e2 + 1)].astype(bf16)

    def stage3_low(e1, carry):
        off = pl.multiple_of(CHUNK * e1, CHUNK)
        t1, t2, t3 = (
            jnp.dot(m3v_ref[e1, p],
                    jnp.concatenate([y2_ref[d0, p, pl.ds(off, CHUNK), :] for d0 in range(16)], axis=0),
                    preferred_element_type=jnp.float32)
            for p in range(3))
        store(e1, t1 - t2, t3 - t1 - t2)
        return carry

    lax.fori_loop(0, half1, stage3_low, 0, unroll=_unroll(half1, 192))

    first_row = lax.broadcasted_iota(jnp.int32, (CHUNK, 1), 0) == 0

    def stage3_high(e1, carry):
        off_a = pl.multiple_of(CHUNK * (r1 - 1 - e1), CHUNK)
        off_b = pl.multiple_of(CHUNK * (r1 - e1), CHUNK)
        t1, t2, t3 = (
            jnp.dot(m3v_ref[e1, p],
                    jnp.concatenate([jnp.where(first_row, y2_ref[d0, p, pl.ds(off_b, CHUNK), :],
                                               y2_ref[d0, p, pl.ds(off_a, CHUNK), :]) for d0 in range(16)], axis=0),
                    preferred_element_type=jnp.float32)
            for p in range(3))
        store(e1, t1 + t2, t1 - t2 - t3)
        return carry

    lax.fori_loop(half1, r1, stage3_high, 0, unroll=_unroll(half1, 192))


def _seq_dft(v):
    b, s, c = v.shape
    r1 = s // 256
    m1, m2, m3, twr, twi = _fft_constants(s)
    m1 = jnp.asarray(m1).astype(jnp.bfloat16)
    m2 = jnp.asarray(m2).astype(jnp.bfloat16)
    blk = pl.BlockSpec((1, s, FFT_LANES), lambda i, j: (i, 0, j))
    const2 = lambda i, j: (0, 0)
    return pl.pallas_call(
        functools.partial(_fft_kernel, r1=r1),
        grid=(b, c // FFT_LANES),
        in_specs=[
            blk,
            pl.BlockSpec(m1.shape, const2),
            pl.BlockSpec(m2.shape, const2),
            pl.BlockSpec(m3.shape, lambda i, j: (0, 0, 0)),
            pl.BlockSpec(twr.shape, const2),
            pl.BlockSpec(twi.shape, const2),
        ],
        out_specs=[blk, blk],
        out_shape=[jax.ShapeDtypeStruct((b, s, c), jnp.bfloat16)] * 2,
        scratch_shapes=[pltpu.VMEM((16, 3, CHUNK * (r1 // 2 + 1), FFT_LANES), jnp.bfloat16),
                        pltpu.VMEM((r1, 3, 256, 256), jnp.bfloat16)],
        compiler_params=pltpu.CompilerParams(
            dimension_semantics=("arbitrary", "arbitrary"), vmem_limit_bytes=VMEM_LIMIT),
        name="seq_dft",
    )(v, m1, m2, jnp.asarray(m3), jnp.asarray(twr), jnp.asarray(twi))


def _main_kernel(x_ref, xp_ref, xn_ref, ur_ref, ui_ref, scale_ref, shift_ref, gate_ref,
                 w_ref, g_ref, wo_ref, cw_ref, cb_ref, lg_ref, lb_ref, o_ref, *, tm):
    i = pl.program_id(1)
    n_tiles = pl.num_programs(1)
    bf16 = jnp.bfloat16
    x = x_ref[0]
    s1 = 1.0 + scale_ref[0]
    sh = shift_ref[0]
    h = x * s1 + sh
    h_halo = jnp.concatenate([xp_ref[0], xn_ref[0]], axis=0) * s1 + sh
    h_ext = jnp.concatenate([h, h_halo], axis=0).astype(bf16)

    hb = h_ext[:tm]
    ur = ur_ref[0]
    ui = ui_ref[0]
    rows = lax.broadcasted_iota(jnp.int32, (tm, 1), 0)
    y_a, y_f = [], []
    for c in range(D_A // COL_BLOCK):
        lo, hi = c * COL_BLOCK, (c + 1) * COL_BLOCK
        w0 = c * 5 * COL_BLOCK
        zc = jnp.dot(h_ext, w_ref[:, w0:w0 + 2 * COL_BLOCK], preferred_element_type=jnp.float32)
        u_all = zc[:, :COL_BLOCK] * zc[:, COL_BLOCK:]
        u = u_all[:tm]
        u_prev = jnp.where(i > 0, u_all[tm + 7:tm + 8], 0.0)
        u_next = jnp.where(i < n_tiles - 1, u_all[tm + 8:tm + 9], 0.0)
        u_m1 = jnp.where(rows == 0, u_prev, pltpu.roll(u, 1, 0))
        u_p1 = jnp.where(rows == tm - 1, u_next, pltpu.roll(u, tm - 1, 0))
        conv = (u_m1 * cw_ref[0:1, lo:hi] + u * cw_ref[1:2, lo:hi] + u_p1 * cw_ref[2:3, lo:hi]
                + cb_ref[:, lo:hi])

        zb = jnp.dot(hb, w_ref[:, w0 + 2 * COL_BLOCK:w0 + 5 * COL_BLOCK], preferred_element_type=jnp.float32)
        bg, ga, gf = (zb[:, k * COL_BLOCK:(k + 1) * COL_BLOCK] for k in range(3))
        y_a.append((bg * conv * jax.nn.silu(ga)).astype(bf16))

        lhs = jnp.concatenate([ur[:, lo:hi], ui[:, lo:hi]], axis=1)
        fr = jnp.dot(lhs, g_ref[c], preferred_element_type=jnp.float32)
        y_f.append((fr * jax.nn.silu(gf)).astype(bf16))

    y = jnp.concatenate(y_a + y_f, axis=1)
    gate = gate_ref[0] * (1.0 / ALPHA)
    for m in range(0, tm, OUT_ROWS):
        o = jnp.dot(y[m:m + OUT_ROWS], wo_ref[...], preferred_element_type=jnp.float32)
        r = x_ref[0, m:m + OUT_ROWS, :] + gate * o
        mu = jnp.mean(r, axis=-1, keepdims=True)
        rc = r - mu
        var = jnp.mean(rc * rc, axis=-1, keepdims=True)
        o_ref[0, m:m + OUT_ROWS, :] = rc * lax.rsqrt(var + LN_EPS / ALPHA ** 2) * lg_ref[...] + lb_ref[...]


def _main(x, ur, ui, mod, row0, w_in, gfold, wo, conv_w, conv_b, ln_g, ln_b, tm):
    b, s, d = x.shape
    n8 = tm // 8
    last8 = s // 8 - 1
    tile = lambda i, j: (i, j, 0)
    const2 = lambda i, j: (0, 0)
    mod_spec = lambda k: pl.BlockSpec((1, 1, d), lambda i, j: (row0 + i, 0, k))
    return pl.pallas_call(
        functools.partial(_main_kernel, tm=tm),
        grid=(b, s // tm),
        in_specs=[
            pl.BlockSpec((1, tm, d), tile),
            pl.BlockSpec((1, 8, d), lambda i, j: (i, jnp.maximum(j * n8 - 1, 0), 0)),
            pl.BlockSpec((1, 8, d), lambda i, j: (i, jnp.minimum((j + 1) * n8, last8), 0)),
            pl.BlockSpec((1, tm, D_F), tile),
            pl.BlockSpec((1, tm, D_F), tile),
            mod_spec(MOD_SCALE),
            mod_spec(MOD_SHIFT),
            mod_spec(MOD_GATE),
            pl.BlockSpec(w_in.shape, const2),
            pl.BlockSpec(gfold.shape, lambda i, j: (0, 0, 0)),
            pl.BlockSpec(wo.shape, const2),
            pl.BlockSpec(conv_w.shape, const2),
            pl.BlockSpec(conv_b.shape, const2),
            pl.BlockSpec(ln_g.shape, const2),
            pl.BlockSpec(ln_b.shape, const2),
        ],
        out_specs=pl.BlockSpec((1, tm, d), tile),
        out_shape=jax.ShapeDtypeStruct((b, s, d), jnp.float32),
        compiler_params=pltpu.CompilerParams(
            dimension_semantics=("arbitrary", "arbitrary"), vmem_limit_bytes=VMEM_LIMIT),
        name="mixer_main",
    )(x, x, x, ur, ui, mod, mod, mod, w_in, gfold, wo, conv_w, conv_b, ln_g, ln_b)


def kernel(x_prompt, x_sample, c_prompt, c_sample, w_ada, b_ada, w_in, conv_w, conv_b,
           w_fmix, w_out, ln_g, ln_b):
    bf16 = jnp.bfloat16
    assert w_in.shape == (1, D_MODEL, 4 * D_A + 2 * D_F) and w_out.shape == (1, D_A + D_F, D_MODEL)
    for x in (x_prompt, x_sample):
        seq = x.shape[1]
        assert x.shape[2] == D_MODEL and seq // 256 in (16, 32) and seq % 256 == 0, x.shape
        assert seq % VF_ROWS == 0 and seq % MAIN_ROWS == 0, x.shape
    w_ada, b_ada, w_in, conv_w, conv_b = w_ada[0], b_ada[0], w_in[0], conv_w[0], conv_b[0]
    w_fmix, w_out, ln_g, ln_b = w_fmix[0], w_out[0], ln_g[0], ln_b[0]

    w_main, w_vf = _regroup_w_in(w_in)
    wo = w_out.astype(bf16)
    gfold = _fold_fmix(w_fmix)

    nb = c_prompt.shape[0]
    mod = _modulation(jnp.concatenate([c_prompt, c_sample], axis=0), w_ada, b_ada)
    mod = mod.reshape(mod.shape[0], 1, 3 * D_MODEL)

    outs = []
    for x, row0 in ((x_prompt, 0), (x_sample, nb)):
        vf = _vf_proj(x, mod, row0, w_vf, tm=VF_ROWS)
        ur, ui = _seq_dft(vf)
        outs.append(_main(x, ur, ui, mod, row0, w_main, gfold, wo, conv_w, conv_b.reshape(1, -1),
                          ln_g.reshape(1, -1), ln_b.reshape(1, -1), tm=MAIN_ROWS))
    return tuple(outs)
```

```python
import functools

import numpy as np
import jax
import jax.numpy as jnp
from jax import lax
from jax.experimental import pallas as pl
from jax.experimental.pallas import tpu as pltpu

D_MODEL = 1024
HEAD_DIM = 128
N_GROUPS = 8
D_A = 1024
D_F = 1024
ALPHA = float(2.0 ** 0.25)
LN_EPS = 1e-5

COL_BG, COL_CG, COL_VA, COL_GA, COL_VF, COL_GF = range(6)
MOD_SHIFT, MOD_SCALE, MOD_GATE = range(3)
VF_ROWS = 2048
MAIN_ROWS = 1024
OUT_ROWS = 256
COL_BLOCK = 256

CHUNK = 16
FFT_LANES = 256
VMEM_LIMIT = 56 * 1024 * 1024
FFT_BLOCK_CYCLES = 8192
E1_GROUP = 8


def _e1_groups(r1):
    n = r1 // 2 // E1_GROUP
    return [E1_GROUP] * (n - 1) + [E1_GROUP + 1]


def _fft_constants(seq):
    r1 = seq // 256
    eye = np.eye(16)
    d = np.arange(16)
    pk = np.where(d[:, None] <= 8, np.cos(2 * np.pi * d[None, :] * d[:, None] / 16),
                  -np.sin(2 * np.pi * d[None, :] * (d[:, None] - 8) / 16))
    m1 = np.einsum('ab,pd->apdb', eye, pk).reshape(256, 256)

    unpack = np.zeros((16, 16), complex)
    unpack[0, 0] = unpack[8, 8] = 1
    for e0 in range(1, 8):
        unpack[e0, e0], unpack[e0, 8 + e0] = 1, 1j
        unpack[16 - e0, e0], unpack[16 - e0, 8 + e0] = 1, -1j
    dd = np.arange(r1)
    lo = np.arange(r1 // 2 + 1)
    ph = dd[None, None, :] * lo[:, None, None] / r1 + dd[None, None, :] * d[None, :, None] / (16.0 * r1)
    c = np.exp(-2j * np.pi * ph)
    m2c = np.einsum('aed,ep->aedp', c, unpack).reshape(16 * len(lo), 16 * r1)
    bounds = np.cumsum([0] + _e1_groups(r1)) * 16
    m2 = np.concatenate([np.concatenate([m2c.real[i:j], m2c.imag[i:j]], axis=0)
                         for i, j in zip(bounds[:-1], bounds[1:])], axis=0)

    ph = (d[None, None, :] * d[:, None, None]) / 16.0 + d[None, None, :] * d[None, :, None] / float(seq)
    c = np.exp(-2j * np.pi * ph) / np.sqrt(seq)
    mirror = np.zeros((16, 16))
    mirror[(16 - d) % 16, d] = 1
    cf = np.einsum('ald,lm->aldm', c, eye).reshape(256, 256)
    cfu = np.einsum('ald,lm->aldm', c, mirror).reshape(256, 256)
    m3 = np.stack([cf.real, cf.imag, cfu.real, cfu.imag])

    tw = np.repeat(np.exp(-2j * np.pi * np.outer(dd, d) / (16.0 * r1)), 16, axis=1)
    f32 = np.float32
    return m1.astype(f32), m2.astype(f32), m3.astype(f32), tw.real.astype(f32), tw.imag.astype(f32)


def _channel_dft():
    n = np.arange(HEAD_DIM)
    ang = 2 * np.pi * np.outer(n, n) / HEAD_DIM
    cs = np.concatenate([np.cos(ang), np.sin(ang)], axis=0) / np.sqrt(HEAD_DIM)
    return cs.astype(np.float32)


def _mod_kernel(c_ref, w_ref, b_ref, o_ref):
    bf16, f32 = jnp.bfloat16, jnp.float32
    n = c_ref.shape[0]
    s = jax.nn.silu(c_ref[...])
    w = w_ref[...]
    s_hi = s.astype(bf16)
    w_hi = w.astype(bf16)
    lhs = jnp.concatenate([s_hi, (s - s_hi.astype(f32)).astype(bf16)], axis=0)
    by_hi = jnp.dot(lhs, w_hi, preferred_element_type=f32)
    by_lo = jnp.dot(lhs, (w - w_hi.astype(f32)).astype(bf16), preferred_element_type=f32)
    o_ref[...] = by_hi[:n] + by_hi[n:] + by_lo[:n] + b_ref[...]


def _fold_pair_kernel(cs_ref, w_ref, o_ref):
    hd = HEAD_DIM
    o_ref[...] = jnp.zeros(o_ref.shape, o_ref.dtype)
    for k in range(2):
        t = jnp.dot(cs_ref[...], w_ref[k], precision=lax.Precision.HIGHEST,
                    preferred_element_type=jnp.float32).astype(o_ref.dtype)
        o_ref[0, k * hd:(k + 1) * hd, k * hd:(k + 1) * hd] = t[:hd]
        o_ref[0, (2 + k) * hd:(3 + k) * hd, k * hd:(k + 1) * hd] = t[hd:]


def _regroup_kernel(cg_ref, va_ref, bg_ref, ga_ref, gf_ref, vf_ref, main_ref, vfo_ref):
    for k, ref in enumerate((cg_ref, va_ref, bg_ref, ga_ref, gf_ref)):
        main_ref[:, k * COL_BLOCK:(k + 1) * COL_BLOCK] = ref[...].astype(main_ref.dtype)
    vfo_ref[...] = vf_ref[...].astype(vfo_ref.dtype)


def _weights_kernel(cg_ref, va_ref, bg_ref, ga_ref, gf_ref, vf_ref, wout_ref, c_ref, wada_ref, bada_ref, cs_ref,
                    wfm_ref, main_ref, vfo_ref, wo_ref, mod_ref, g_ref):
    _regroup_kernel(cg_ref, va_ref, bg_ref, ga_ref, gf_ref, vf_ref, main_ref, vfo_ref)
    wo_ref[...] = wout_ref[...].astype(wo_ref.dtype)
    _mod_kernel(c_ref, wada_ref, bada_ref, mod_ref)
    _fold_pair_kernel(cs_ref, wfm_ref, g_ref)


def _prepare_weights(c_all, w_ada, b_ada, w_in, w_fmix, w_out):
    d = w_in.shape[0]
    n = c_all.shape[0]
    n_blocks = D_A // COL_BLOCK
    assert N_GROUPS == 2 * n_blocks
    t_mod = 3 * D_MODEL // n_blocks
    t_out = D_MODEL // n_blocks
    bf16 = jnp.bfloat16
    src = lambda k: pl.BlockSpec((d, COL_BLOCK), lambda c: (0, k * n_blocks + c))
    col = lambda rows, width: pl.BlockSpec((rows, width), lambda c: (0, c))
    return pl.pallas_call(
        _weights_kernel,
        grid=(n_blocks,),
        in_specs=[src(k) for k in (COL_CG, COL_VA, COL_BG, COL_GA, COL_GF, COL_VF)] + [
            col(D_A + D_F, t_out),
            pl.BlockSpec((n, d), lambda c: (0, 0)),
            col(d, t_mod),
            col(1, t_mod),
            pl.BlockSpec((2 * HEAD_DIM, HEAD_DIM), lambda c: (0, 0)),
            pl.BlockSpec((2, HEAD_DIM, HEAD_DIM), lambda c: (c, 0, 0)),
        ],
        out_specs=[col(d, 5 * COL_BLOCK), col(d, COL_BLOCK), col(D_A + D_F, t_out), col(n, t_mod),
                   pl.BlockSpec((1, 4 * HEAD_DIM, 2 * HEAD_DIM), lambda c: (c, 0, 0))],
        out_shape=[jax.ShapeDtypeStruct((d, 5 * D_A), bf16),
                   jax.ShapeDtypeStruct((d, D_F), bf16),
                   jax.ShapeDtypeStruct((D_A + D_F, D_MODEL), bf16),
                   jax.ShapeDtypeStruct((n, 3 * D_MODEL), jnp.float32),
                   jax.ShapeDtypeStruct((n_blocks, 4 * HEAD_DIM, 2 * HEAD_DIM), bf16)],
        compiler_params=pltpu.CompilerParams(dimension_semantics=("arbitrary",), vmem_limit_bytes=VMEM_LIMIT),
        name="weights_prep",
    )(w_in, w_in, w_in, w_in, w_in, w_in, w_out, c_all, w_ada, b_ada.reshape(1, -1),
      jnp.asarray(_channel_dft()), w_fmix)


def _vf_kernel(x_ref, scale_ref, shift_ref, w_ref, o_ref):
    h = x_ref[0] * (1.0 + scale_ref[0]) + shift_ref[0]
    o_ref[0] = jnp.dot(h.astype(jnp.bfloat16), w_ref[...],
                       preferred_element_type=jnp.float32).astype(o_ref.dtype)


def _vf_proj(x, mod, row0, w_in, tm):
    b, s, d = x.shape
    return pl.pallas_call(
        _vf_kernel,
        grid=(b, s // tm),
        in_specs=[
            pl.BlockSpec((1, tm, d), lambda i, j: (i, j, 0)),
            pl.BlockSpec((1, 1, d), lambda i, j: (row0 + i, 0, MOD_SCALE)),
            pl.BlockSpec((1, 1, d), lambda i, j: (row0 + i, 0, MOD_SHIFT)),
            pl.BlockSpec((d, D_F), lambda i, j: (0, 0)),
        ],
        out_specs=pl.BlockSpec((1, tm, D_F), lambda i, j: (i, j, 0)),
        out_shape=jax.ShapeDtypeStruct((b, s, D_F), jnp.bfloat16),
        compiler_params=pltpu.CompilerParams(
            dimension_semantics=("arbitrary", "arbitrary"), vmem_limit_bytes=VMEM_LIMIT),
        name="vf_proj",
    )(x, mod, mod, w_in)


def _unroll(trips, mxu_cycles):
    want = max(1, min(trips, FFT_BLOCK_CYCLES // mxu_cycles))
    return max(u for u in range(1, want + 1) if trips % u == 0)


def _fft_kernel(v_ref, m1_ref, m2_ref, m3_ref, twr_ref, twi_ref, ur_ref, ui_ref, y2_ref, m3v_ref, *, r1):
    k2 = CHUNK * r1
    half1 = r1 // 2
    bf16 = jnp.bfloat16

    @pl.when((pl.program_id(0) == 0) & (pl.program_id(1) == 0))
    def _():
        def fold(base, sign):
            def body(e1, carry):
                wr = twr_ref[pl.ds(e1, 1), :]
                wi = twi_ref[pl.ds(e1, 1), :]
                cr = m3_ref[base] * wr - m3_ref[base + 1] * wi
                ci = m3_ref[base] * wi + m3_ref[base + 1] * wr
                for p, val in enumerate((cr, ci, cr + sign * ci)):
                    m3v_ref[e1, p] = val.astype(bf16)
                return carry
            return body
        lax.fori_loop(0, half1, fold(0, 1.0), 0)
        lax.fori_loop(half1, r1, fold(2, -1.0), 0)

    def stage1(d1, carry):
        rhs = jnp.concatenate(
            [v_ref[0, pl.ds(pl.multiple_of(k2 * d2 + CHUNK * d1, CHUNK), CHUNK), :] for d2 in range(16)], axis=0)
        r = jnp.dot(m1_ref[...], rhs, preferred_element_type=jnp.float32)
        ur_ref[0, pl.ds(pl.multiple_of(256 * d1, 256), 256), :] = r.astype(bf16)
        return carry

    lax.fori_loop(0, r1, stage1, 0, unroll=_unroll(r1, 64))

    def stage2(d0, carry):
        off = pl.multiple_of(CHUNK * d0, CHUNK)
        rhs = jnp.concatenate([ur_ref[0, pl.ds(256 * d1 + off, CHUNK), :] for d1 in range(r1)], axis=0)
        row = 0
        for n_e1 in _e1_groups(r1):
            rows = CHUNK * n_e1
            r = jnp.dot(m2_ref[2 * row:2 * (row + rows), :], rhs, preferred_element_type=jnp.float32)
            yr, yi = r[:rows], r[rows:]
            for p, val in enumerate((yr, yi, yr + yi)):
                y2_ref[d0, p, row:row + rows, :] = val.astype(bf16)
            row += rows
        return carry

    lax.fori_loop(0, 16, stage2, 0, unroll=_unroll(16, r1 * r1 // 4 + 64))

    def store(e1, re, im):
        off = pl.multiple_of(CHUNK * e1, CHUNK)
        for e2 in range(16):
            row = pl.multiple_of(k2 * e2 + off, CHUNK)
            ur_ref[0, pl.ds(row, CHUNK), :] = re[CHUNK * e2:CHUNK * (e2 + 1)].astype(bf16)
            ui_ref[0, pl.ds(row, CHUNK), :] = im[CHUNK * e2:CHUNK * (e2 + 1)].astype(bf16)

    def stage3_low(e1, carry):
        off = pl.multiple_of(CHUNK * e1, CHUNK)
        t1, t2, t3 = (
            jnp.dot(m3v_ref[e1, p],
                    jnp.concatenate([y2_ref[d0, p, pl.ds(off, CHUNK), :] for d0 in range(16)], axis=0),
                    preferred_element_type=jnp.float32)
            for p in range(3))
        store(e1, t1 - t2, t3 - t1 - t2)
        return carry

    lax.fori_loop(0, half1, stage3_low, 0, unroll=_unroll(half1, 192))

    first_row = lax.broadcasted_iota(jnp.int32, (CHUNK, 1), 0) == 0

    def stage3_high(e1, carry):
        off_a = pl.multiple_of(CHUNK * (r1 - 1 - e1), CHUNK)
        off_b = pl.multiple_of(CHUNK * (r1 - e1), CHUNK)
        t1, t2, t3 = (
            jnp.dot(m3v_ref[e1, p],
                    jnp.concatenate([jnp.where(first_row, y2_ref[d0, p, pl.ds(off_b, CHUNK), :],
                                               y2_ref[d0, p, pl.ds(off_a, CHUNK), :]) for d0 in range(16)], axis=0),
                    preferred_element_type=jnp.float32)
            for p in range(3))
        store(e1, t1 + t2, t1 - t2 - t3)
        return carry

    lax.fori_loop(half1, r1, stage3_high, 0, unroll=_unroll(half1, 192))


def _seq_dft(v):
    b, s, c = v.shape
    r1 = s // 256
    m1, m2, m3, twr, twi = _fft_constants(s)
    m1 = jnp.asarray(m1).astype(jnp.bfloat16)
    m2 = jnp.asarray(m2).astype(jnp.bfloat16)
    blk = pl.BlockSpec((1, s, FFT_LANES), lambda i, j: (i, 0, j))
    const2 = lambda i, j: (0, 0)
    return pl.pallas_call(
        functools.partial(_fft_kernel, r1=r1),
        grid=(b, c // FFT_LANES),
        in_specs=[
            blk,
            pl.BlockSpec(m1.shape, const2),
            pl.BlockSpec(m2.shape, const2),
            pl.BlockSpec(m3.shape, lambda i, j: (0, 0, 0)),
            pl.BlockSpec(twr.shape, const2),
            pl.BlockSpec(twi.shape, const2),
        ],
        out_specs=[blk, blk],
        out_shape=[jax.ShapeDtypeStruct((b, s, c), jnp.bfloat16)] * 2,
        scratch_shapes=[pltpu.VMEM((16, 3, CHUNK * (r1 // 2 + 1), FFT_LANES), jnp.bfloat16),
                        pltpu.VMEM((r1, 3, 256, 256), jnp.bfloat16)],
        compiler_params=pltpu.CompilerParams(
            dimension_semantics=("arbitrary", "arbitrary"), vmem_limit_bytes=VMEM_LIMIT),
        name="seq_dft",
    )(v, m1, m2, jnp.asarray(m3), jnp.asarray(twr), jnp.asarray(twi))


def _main_kernel(x_ref, xp_ref, xn_ref, ur_ref, ui_ref, scale_ref, shift_ref, gate_ref,
                 w_ref, g_ref, wo_ref, cw_ref, cb_ref, lg_ref, lb_ref, o_ref, *, tm):
    i = pl.program_id(1)
    n_tiles = pl.num_programs(1)
    bf16 = jnp.bfloat16
    x = x_ref[0]
    s1 = 1.0 + scale_ref[0]
    sh = shift_ref[0]
    h = x * s1 + sh
    h_halo = jnp.concatenate([xp_ref[0], xn_ref[0]], axis=0) * s1 + sh
    h_ext = jnp.concatenate([h, h_halo], axis=0).astype(bf16)

    hb = h_ext[:tm]
    ur = ur_ref[0]
    ui = ui_ref[0]
    rows = lax.broadcasted_iota(jnp.int32, (tm, 1), 0)
    y_a, y_f = [], []
    for c in range(D_A // COL_BLOCK):
        lo, hi = c * COL_BLOCK, (c + 1) * COL_BLOCK
        w0 = c * 5 * COL_BLOCK
        zc = jnp.dot(h_ext, w_ref[:, w0:w0 + 2 * COL_BLOCK], preferred_element_type=jnp.float32)
        u_all = zc[:, :COL_BLOCK] * zc[:, COL_BLOCK:]
        u = u_all[:tm]
        u_prev = jnp.where(i > 0, u_all[tm + 7:tm + 8], 0.0)
        u_next = jnp.where(i < n_tiles - 1, u_all[tm + 8:tm + 9], 0.0)
        u_m1 = jnp.where(rows == 0, u_prev, pltpu.roll(u, 1, 0))
        u_p1 = jnp.where(rows == tm - 1, u_next, pltpu.roll(u, tm - 1, 0))
        conv = (u_m1 * cw_ref[0:1, lo:hi] + u * cw_ref[1:2, lo:hi] + u_p1 * cw_ref[2:3, lo:hi]
                + cb_ref[:, lo:hi])

        zb = jnp.dot(hb, w_ref[:, w0 + 2 * COL_BLOCK:w0 + 5 * COL_BLOCK], preferred_element_type=jnp.float32)
        bg, ga, gf = (zb[:, k * COL_BLOCK:(k + 1) * COL_BLOCK] for k in range(3))
        y_a.append((bg * conv * jax.nn.silu(ga)).astype(bf16))

        lhs = jnp.concatenate([ur[:, lo:hi], ui[:, lo:hi]], axis=1)
        fr = jnp.dot(lhs, g_ref[c], preferred_element_type=jnp.float32)
        y_f.append((fr * jax.nn.silu(gf)).astype(bf16))

    y = jnp.concatenate(y_a + y_f, axis=1)
    gate = gate_ref[0] * (1.0 / ALPHA)
    for m in range(0, tm, OUT_ROWS):
        o = jnp.dot(y[m:m + OUT_ROWS], wo_ref[...], preferred_element_type=jnp.float32)
        r = x_ref[0, m:m + OUT_ROWS, :] + gate * o
        mu = jnp.mean(r, axis=-1, keepdims=True)
        rc = r - mu
        var = jnp.mean(rc * rc, axis=-1, keepdims=True)
        o_ref[0, m:m + OUT_ROWS, :] = rc * lax.rsqrt(var + LN_EPS / ALPHA ** 2) * lg_ref[...] + lb_ref[...]


def _main(x, ur, ui, mod, row0, w_in, gfold, wo, conv_w, conv_b, ln_g, ln_b, tm):
    b, s, d = x.shape
    n8 = tm // 8
    last8 = s // 8 - 1
    tile = lambda i, j: (i, j, 0)
    const2 = lambda i, j: (0, 0)
    mod_spec = lambda k: pl.BlockSpec((1, 1, d), lambda i, j: (row0 + i, 0, k))
    return pl.pallas_call(
        functools.partial(_main_kernel, tm=tm),
        grid=(b, s // tm),
        in_specs=[
            pl.BlockSpec((1, tm, d), tile),
            pl.BlockSpec((1, 8, d), lambda i, j: (i, jnp.maximum(j * n8 - 1, 0), 0)),
            pl.BlockSpec((1, 8, d), lambda i, j: (i, jnp.minimum((j + 1) * n8, last8), 0)),
            pl.BlockSpec((1, tm, D_F), tile),
            pl.BlockSpec((1, tm, D_F), tile),
            mod_spec(MOD_SCALE),
            mod_spec(MOD_SHIFT),
            mod_spec(MOD_GATE),
            pl.BlockSpec(w_in.shape, const2),
            pl.BlockSpec(gfold.shape, lambda i, j: (0, 0, 0)),
            pl.BlockSpec(wo.shape, const2),
            pl.BlockSpec(conv_w.shape, const2),
            pl.BlockSpec(conv_b.shape, const2),
            pl.BlockSpec(ln_g.shape, const2),
            pl.BlockSpec(ln_b.shape, const2),
        ],
        out_specs=pl.BlockSpec((1, tm, d), tile),
        out_shape=jax.ShapeDtypeStruct((b, s, d), jnp.float32),
        compiler_params=pltpu.CompilerParams(
            dimension_semantics=("arbitrary", "arbitrary"), vmem_limit_bytes=VMEM_LIMIT),
        name="mixer_main",
    )(x, x, x, ur, ui, mod, mod, mod, w_in, gfold, wo, conv_w, conv_b, ln_g, ln_b)


def kernel(x_prompt, x_sample, c_prompt, c_sample, w_ada, b_ada, w_in, conv_w, conv_b,
           w_fmix, w_out, ln_g, ln_b):
    assert w_in.shape == (1, D_MODEL, 4 * D_A + 2 * D_F) and w_out.shape == (1, D_A + D_F, D_MODEL)
    for x in (x_prompt, x_sample):
        seq = x.shape[1]
        assert x.shape[2] == D_MODEL and seq // 256 in (16, 32) and seq % 256 == 0, x.shape
        assert seq % VF_ROWS == 0 and seq % MAIN_ROWS == 0, x.shape
    w_ada, b_ada, w_in, conv_w, conv_b = w_ada[0], b_ada[0], w_in[0], conv_w[0], conv_b[0]
    w_fmix, w_out, ln_g, ln_b = w_fmix[0], w_out[0], ln_g[0], ln_b[0]

    nb = c_prompt.shape[0]
    w_main, w_vf, wo, mod, gfold = _prepare_weights(
        jnp.concatenate([c_prompt, c_sample], axis=0), w_ada, b_ada, w_in, w_fmix, w_out)
    mod = mod.reshape(mod.shape[0], 1, 3 * D_MODEL)

    outs = []
    for x, row0 in ((x_prompt, 0), (x_sample, nb)):
        vf = _vf_proj(x, mod, row0, w_vf, tm=VF_ROWS)
        ur, ui = _seq_dft(vf)
        outs.append(_main(x, ur, ui, mod, row0, w_main, gfold, wo, conv_w, conv_b.reshape(1, -1),
                          ln_g.reshape(1, -1), ln_b.reshape(1, -1), tm=MAIN_ROWS))
    return tuple(outs)
```
